```python
import jax, jax.numpy as jnp
from jax import lax
import numpy as np

D_MODEL = 1024
BATCH = 8
SEQ = 4096
DEPTH = 2

A_WIDTH = D_MODEL // 2
B_WIDTH = D_MODEL - A_WIDTH
A_HEADS = 8
B_HEADS = 8
A_CONV = 31
B_CONV = 3
IN_WIDTH = 2 * A_WIDTH + 3 * B_WIDTH
POOL_WINDOWS = (2, 4, 8, 16)
POOL_GROUPS = len(POOL_WINDOWS)
POOL_GROUP = D_MODEL // POOL_GROUPS
D_FF = ((8 * D_MODEL // 3 + 255) // 256) * 256
FFN_CONV = 3
RMS_EPS = 1e-6
LN_EPS = 1e-5
N_EVEN = (DEPTH + 1) // 2
N_ODD = DEPTH // 2

kernel_name = "hybrid_conv_pool_convffn_trunk"


def rmsnorm(x, g):
    xf = x.astype(jnp.float32)
    y = xf * lax.rsqrt(jnp.mean(xf * xf, axis=-1, keepdims=True) + RMS_EPS)
    return (y * g.astype(jnp.float32)).astype(x.dtype)


def layernorm(x, g, b):
    xf = x.astype(jnp.float32)
    mu = jnp.mean(xf, axis=-1, keepdims=True)
    xc = xf - mu
    var = jnp.mean(xc * xc, axis=-1, keepdims=True)
    y = xc * lax.rsqrt(var + LN_EPS) * g.astype(jnp.float32) + b.astype(jnp.float32)
    return y.astype(x.dtype)


def causal_dwconv(x, w):
    k, c = w.shape
    return lax.conv_general_dilated(
        x, w[:, None, :].astype(x.dtype), window_strides=(1,), padding=[(k - 1, 0)],
        dimension_numbers=("NWC", "WIO", "NWC"), feature_group_count=c)


def conv_mixer(h, w_in, conv_a, ln_a_g, ln_a_b, conv_b, w_out):
    z = h @ w_in
    a_val, a_gate, b_gate, c_gate, bc_val = jnp.split(
        z, [A_WIDTH, 2 * A_WIDTH, 2 * A_WIDTH + B_WIDTH, 2 * A_WIDTH + 2 * B_WIDTH], axis=-1)
    a = causal_dwconv(a_val * jax.nn.sigmoid(a_gate), conv_a)
    a = jax.nn.silu(layernorm(a, ln_a_g, ln_a_b))
    b = b_gate * causal_dwconv(c_gate * bc_val, conv_b)
    return jnp.concatenate([a, b], axis=-1) @ w_out


def pool_mixer(h, w_pool, pool_scale):
    s = h.shape[1]
    hf = h.astype(jnp.float32)
    cs = jnp.cumsum(hf, axis=1)
    t = jnp.arange(1, s + 1, dtype=jnp.float32)[:, None]
    outs = []
    for g, w in enumerate(POOL_WINDOWS):
        sl = slice(g * POOL_GROUP, (g + 1) * POOL_GROUP)
        c = cs[..., sl]
        prev = jnp.pad(c, ((0, 0), (w, 0), (0, 0)))[:, :s]
        mean = (c - prev) / jnp.minimum(t, float(w))
        outs.append(mean - hf[..., sl])
    p = jnp.stack(outs, axis=2).astype(h.dtype)
    y = jnp.einsum("bsgc,gcd->bsgd", p, w_pool).reshape(h.shape)
    return y * pool_scale


def conv_ffn(h, w_up, w_conv, w_down):
    u = causal_dwconv(h @ w_up, w_conv)
    g, v = jnp.split(u, 2, axis=-1)
    return (jax.nn.silu(g) * v) @ w_down


def setup_inputs(seed: int = 0) -> dict:
    key = jax.random.key(seed)
    ks = jax.random.split(key, 20)
    f32 = jnp.float32
    nrm = lambda k, shape, scale: jax.random.normal(k, shape, f32) * scale
    return {
        "x": nrm(ks[0], (BATCH, SEQ, D_MODEL), 1.0),
        "norm_mix_even": 1.0 + nrm(ks[1], (N_EVEN, D_MODEL), 0.02),
        "w_in": nrm(ks[2], (N_EVEN, D_MODEL, IN_WIDTH), D_MODEL ** -0.5),
        "conv_a": nrm(ks[3], (N_EVEN, A_CONV, A_WIDTH), A_CONV ** -0.5),
        "ln_a_g": 1.0 + nrm(ks[4], (N_EVEN, A_WIDTH), 0.02),
        "ln_a_b": nrm(ks[5], (N_EVEN, A_WIDTH), 0.02),
        "conv_b": nrm(ks[6], (N_EVEN, B_CONV, B_WIDTH), B_CONV ** -0.5),
        "w_out": nrm(ks[7], (N_EVEN, D_MODEL, D_MODEL), D_MODEL ** -0.5),
        "norm_mix_odd": 1.0 + nrm(ks[8], (N_ODD, D_MODEL), 0.02),
        "w_pool": nrm(ks[9], (N_ODD, POOL_GROUPS, POOL_GROUP, POOL_GROUP), POOL_GROUP ** -0.5),
        "pool_scale": 1.0 + nrm(ks[10], (N_ODD, D_MODEL), 0.1),
        "norm_ffn": 1.0 + nrm(ks[11], (DEPTH, D_MODEL), 0.02),
        "w_up": nrm(ks[12], (DEPTH, D_MODEL, 2 * D_FF), D_MODEL ** -0.5),
        "conv_ffn_w": nrm(ks[13], (DEPTH, FFN_CONV, 2 * D_FF), FFN_CONV ** -0.5),
        "w_down": nrm(ks[14], (DEPTH, D_FF, D_MODEL), D_FF ** -0.5),
        "norm_final": 1.0 + nrm(ks[15], (D_MODEL,), 0.02),
    }


def reference(x, norm_mix_even, w_in, conv_a, ln_a_g, ln_a_b, conv_b, w_out,
              norm_mix_odd, w_pool, pool_scale, norm_ffn, w_up, conv_ffn_w, w_down,
              norm_final):
    for layer in range(DEPTH):
        i = layer // 2
        if layer % 2 == 0:
            h = rmsnorm(x, norm_mix_even[i])
            x = x + conv_mixer(h, w_in[i], conv_a[i], ln_a_g[i], ln_a_b[i], conv_b[i], w_out[i])
        else:
            h = rmsnorm(x, norm_mix_odd[i])
            x = x + pool_mixer(h, w_pool[i], pool_scale[i])
        x = x + conv_ffn(rmsnorm(x, norm_ffn[layer]), w_up[layer], conv_ffn_w[layer], w_down[layer])
    return rmsnorm(x, norm_final)
```

```python
import functools

import jax
import jax.numpy as jnp
from jax import lax
from jax.experimental import pallas as pl
from jax.experimental.pallas import tpu as pltpu

D_MODEL = 1024
A_WIDTH = 512
B_WIDTH = 512
A_CONV = 31
IN_WIDTH = 2 * A_WIDTH + 3 * B_WIDTH
POOL_WINDOWS = (2, 4, 8, 16)
POOL_GROUP = 256
D_FF = 2816
RMS_EPS = 1e-6
LN_EPS = 1e-5

SUBLANES = 8
LANES = 128
SEQ_TILE = 512
FF_CHUNK = 256
N_FF_CHUNKS = D_FF // FF_CHUNK
A_HALO = 32
A_ROWS = 64
VMEM_LIMIT = 56 * 1024 * 1024

_dot = functools.partial(jnp.dot, preferred_element_type=jnp.float32)


def _rms(x, g):
    ms = jnp.mean(x * x, axis=-1, keepdims=True)
    return x * lax.rsqrt(ms + RMS_EPS) * g


def _sigmoid(x):
    return 1.0 / (1.0 + jnp.exp(-x))


def _shift_rows(a, r, prev):
    if r == SUBLANES:
        return jnp.concatenate([prev, a[:-SUBLANES]], axis=0)
    rolled = pltpu.roll(a, r, axis=0)
    rows = lax.broadcasted_iota(jnp.int32, prev.shape, 0)
    head = jnp.where(rows < r, pltpu.roll(prev, r, axis=0), rolled[:SUBLANES])
    return jnp.concatenate([head, rolled[SUBLANES:]], axis=0)


def _conv3(u, w, prev):
    return w[2:3] * u + w[1:2] * _shift_rows(u, 1, prev) + w[0:1] * _shift_rows(u, 2, prev)


def _mixer0_kernel(x_ref, g_ref, w_in_ref, ca_ref, lng_ref, lnb_ref, cb_ref, w_out_ref, o_ref,
                   glu_buf, cv_carry, ab_buf):
    t = SEQ_TILE

    @pl.when(pl.program_id(1) == 0)
    def _():
        glu_buf[0:A_HALO, :] = jnp.zeros((A_HALO, A_WIDTH), jnp.float32)
        cv_carry[...] = jnp.zeros_like(cv_carry)

    x = x_ref[0]
    h = _rms(x, g_ref[...]).astype(jnp.bfloat16)

    def zcol(i):
        return _dot(h, w_in_ref[:, i * A_WIDTH:(i + 1) * A_WIDTH])

    glu_buf[A_HALO:A_HALO + t, :] = zcol(0) * _sigmoid(zcol(1))

    cv = zcol(3) * zcol(4)
    conv_b = _conv3(cv, cb_ref[...], cv_carry[...])
    cv_carry[...] = cv[t - SUBLANES:]
    ab_buf[:, A_WIDTH:] = (zcol(2) * conv_b).astype(jnp.bfloat16)

    def rows_body(i, carry):
        t0 = pl.multiple_of(i * A_ROWS, A_ROWS)
        cols = []
        for c in range(A_WIDTH // LANES):
            lanes = slice(c * LANES, (c + 1) * LANES)
            slab = glu_buf[pl.ds(t0, A_ROWS + A_HALO), lanes]
            acc = None
            for r in range(SUBLANES):
                rolled = slab if r == 0 else pltpu.roll(slab, r, axis=0)
                for q in range(A_HALO // SUBLANES):
                    back = SUBLANES * q + r
                    if back >= A_CONV:
                        continue
                    k = A_CONV - 1 - back
                    lo = A_HALO - SUBLANES * q
                    term = ca_ref[k:k + 1, lanes] * rolled[lo:lo + A_ROWS]
                    acc = term if acc is None else acc + term
            cols.append(acc)
        a = jnp.concatenate(cols, axis=1)
        mu = jnp.mean(a, axis=-1, keepdims=True)
        ac = a - mu
        var = jnp.mean(ac * ac, axis=-1, keepdims=True)
        y = ac * lax.rsqrt(var + LN_EPS) * lng_ref[...] + lnb_ref[...]
        ab_buf[pl.ds(t0, A_ROWS), 0:A_WIDTH] = (y * _sigmoid(y)).astype(jnp.bfloat16)
        return carry

    lax.fori_loop(0, t // A_ROWS, rows_body, 0)
    glu_buf[0:A_HALO, :] = glu_buf[t:t + A_HALO, :]

    o_ref[0] = x + _dot(ab_buf[...], w_out_ref[...])


def _pool_kernel(x_ref, g_ref, wp_ref, ps_ref, o_ref, carry):
    t = SEQ_TILE
    s = pl.program_id(1)

    @pl.when(s == 0)
    def _():
        carry[...] = jnp.zeros_like(carry)

    x = x_ref[0]
    h = _rms(x, g_ref[...])
    pos = s * t + lax.broadcasted_iota(jnp.int32, (t, POOL_GROUP), 0)
    cur = h
    outs = []
    for lvl, w in enumerate(POOL_WINDOWS):
        lo = lvl * POOL_GROUP
        r = w // 2
        nxt = cur + _shift_rows(cur, r, carry[lvl, :, lo:])
        carry[lvl, :, lo:] = cur[t - SUBLANES:]
        cnt = jnp.minimum(pos + 1, w).astype(jnp.float32)
        p = nxt[:, :POOL_GROUP] / cnt - h[:, lo:lo + POOL_GROUP]
        outs.append(_dot(p.astype(jnp.bfloat16), wp_ref[lvl]))
        cur = nxt[:, POOL_GROUP:]
    o_ref[0] = x + jnp.concatenate(outs, axis=1) * ps_ref[...]


def _ffn_kernel(x_ref, g_ref, w_up_ref, cw_ref, w_down_ref, gf_ref, o_ref, hb, acc_ref, carry,
                *, final_norm):
    t = SEQ_TILE

    @pl.when(pl.program_id(1) == 0)
    def _():
        carry[...] = jnp.zeros_like(carry)

    x = x_ref[0]
    hb[...] = _rms(x, g_ref[...]).astype(jnp.bfloat16)

    def conv_half(idx):
        u = _dot(hb[...], w_up_ref[idx])
        out = _conv3(u, cw_ref[idx], carry[idx])
        carry[idx] = u[t - SUBLANES:]
        return out

    for j in range(N_FF_CHUNKS):
        gate = conv_half(j)
        val = conv_half(N_FF_CHUNKS + j)
        act = (gate * _sigmoid(gate) * val).astype(jnp.bfloat16)
        contrib = _dot(act, w_down_ref[j])
        if j == 0:
            acc_ref[...] = x + contrib
        else:
            acc_ref[...] += contrib

    y = acc_ref[...]
    if final_norm:
        y = _rms(y, gf_ref[...])
    o_ref[0] = y


def _const_spec(shape):
    return pl.BlockSpec(shape, lambda b, s: (0,) * len(shape), pipeline_mode=pl.Buffered(1))


def _tile_spec():
    return pl.BlockSpec((1, SEQ_TILE, D_MODEL), lambda b, s: (b, s, 0))


def _call(body, x, consts, scratch, name):
    batch, seq, _ = x.shape
    return pl.pallas_call(
        body,
        grid=(batch, seq // SEQ_TILE),
        in_specs=[_tile_spec()] + [_const_spec(c.shape) for c in consts],
        out_specs=_tile_spec(),
        out_shape=jax.ShapeDtypeStruct(x.shape, x.dtype),
        scratch_shapes=scratch,
        compiler_params=pltpu.CompilerParams(
            dimension_semantics=("arbitrary", "arbitrary"), vmem_limit_bytes=VMEM_LIMIT),
        name=name,
    )(x, *consts)


def _mixer0(x, g, w_in, conv_a, ln_g, ln_b, conv_b, w_out):
    consts = [g.reshape(1, -1), w_in.astype(jnp.bfloat16), conv_a, ln_g.reshape(1, -1),
              ln_b.reshape(1, -1), conv_b, w_out.astype(jnp.bfloat16)]
    scratch = [pltpu.VMEM((A_HALO + SEQ_TILE, A_WIDTH), jnp.float32),
               pltpu.VMEM((SUBLANES, B_WIDTH), jnp.float32),
               pltpu.VMEM((SEQ_TILE, D_MODEL), jnp.bfloat16)]
    return _call(_mixer0_kernel, x, consts, scratch, "mixer0")


def _pool(x, g, w_pool, pool_scale):
    consts = [g.reshape(1, -1), w_pool.astype(jnp.bfloat16), pool_scale.reshape(1, -1)]
    scratch = [pltpu.VMEM((len(POOL_WINDOWS), SUBLANES, D_MODEL), jnp.float32)]
    return _call(_pool_kernel, x, consts, scratch, "pool_mixer")


def _ffn(x, g, w_up, conv_w, w_down, g_final, final_norm, name):
    w_up_c = w_up.astype(jnp.bfloat16).reshape(D_MODEL, 2 * N_FF_CHUNKS, FF_CHUNK).transpose(1, 0, 2)
    conv_c = conv_w.reshape(3, 2 * N_FF_CHUNKS, FF_CHUNK).transpose(1, 0, 2)
    w_down_c = w_down.astype(jnp.bfloat16).reshape(N_FF_CHUNKS, FF_CHUNK, D_MODEL)
    consts = [g.reshape(1, -1), w_up_c, conv_c, w_down_c, g_final.reshape(1, -1)]
    scratch = [pltpu.VMEM((SEQ_TILE, D_MODEL), jnp.bfloat16),
               pltpu.VMEM((SEQ_TILE, D_MODEL), jnp.float32),
               pltpu.VMEM((2 * N_FF_CHUNKS, SUBLANES, FF_CHUNK), jnp.float32)]
    return _call(functools.partial(_ffn_kernel, final_norm=final_norm), x, consts, scratch, name)


def kernel(x, norm_mix_even, w_in, conv_a, ln_a_g, ln_a_b, conv_b, w_out, norm_mix_odd, w_pool, pool_scale, norm_ffn, w_up, conv_ffn_w, w_down, norm_final):
    x = _mixer0(x, norm_mix_even[0], w_in[0], conv_a[0], ln_a_g[0], ln_a_b[0], conv_b[0], w_out[0])
    x = _ffn(x, norm_ffn[0], w_up[0], conv_ffn_w[0], w_down[0], norm_final, False, "ffn0")
    x = _pool(x, norm_mix_odd[0], w_pool[0], pool_scale[0])
    x = _ffn(x, norm_ffn[1], w_up[1], conv_ffn_w[1], w_down[1], norm_final, True, "ffn1")
    return x
```

```python
import functools

import jax
import jax.numpy as jnp
from jax import lax
from jax.experimental import pallas as pl
from jax.experimental.pallas import tpu as pltpu

D_MODEL = 1024
A_WIDTH = 512
B_WIDTH = 512
A_CONV = 31
POOL_WINDOWS = (2, 4, 8, 16)
POOL_GROUP = 256
D_FF = 2816
FFN_CONV = 3
RMS_EPS = 1e-6
LN_EPS = 1e-5

SUBLANES = 8
LANES = 128
SEQ_TILE = 512
SEG = SEQ_TILE // SUBLANES
FF_CHUNK = 256
N_FF_CHUNKS = D_FF // FF_CHUNK
FF_ROWS = 64
A_ROWS = 64
A_BACK = (A_CONV - 1) * SUBLANES
C3_BACK = (FFN_CONV - 1) * SUBLANES
POOL_BACK = (POOL_WINDOWS[-1] // 2) * SUBLANES
VMEM_LIMIT = 56 * 1024 * 1024

_dot = functools.partial(jnp.dot, preferred_element_type=jnp.float32)


def _rms(x, g):
    ms = jnp.mean(x * x, axis=-1, keepdims=True)
    return x * lax.rsqrt(ms + RMS_EPS) * g


def _sigmoid(x):
    return 1.0 / (1.0 + jnp.exp(-x))


def _history_rows(cur_tail, prev_tail):
    rows = lax.broadcasted_iota(jnp.int32, (SUBLANES, cur_tail.shape[1]), 0)
    groups = []
    for i in range(cur_tail.shape[0] // SUBLANES):
        sl = slice(i * SUBLANES, (i + 1) * SUBLANES)
        groups.append(jnp.where(rows == 0, pltpu.roll(prev_tail[sl], 1, axis=0),
                                pltpu.roll(cur_tail[sl], 1, axis=0)))
    return groups[0] if len(groups) == 1 else jnp.concatenate(groups, axis=0)


def _mixer0_kernel(x_ref, g_ref, w_in_ref, ca_ref, lng_ref, lnb_ref, cb_ref, w_out_ref, o_ref,
                   xp, glu_buf, glu_carry, cv_buf, cv_carry, ab_buf):
    t = SEQ_TILE

    @pl.when(pl.program_id(1) == 0)
    def _():
        glu_carry[...] = jnp.zeros_like(glu_carry)
        cv_carry[...] = jnp.zeros_like(cv_carry)

    for b in range(SEG):
        xp[b * SUBLANES:(b + 1) * SUBLANES, :] = x_ref[0, 0, :, b, :]
    x = xp[...]
    h = _rms(x, g_ref[...]).astype(jnp.bfloat16)

    def zcol(i):
        return _dot(h, w_in_ref[:, i * A_WIDTH:(i + 1) * A_WIDTH])

    glu = zcol(0) * _sigmoid(zcol(1))
    glu_buf[A_BACK:A_BACK + t, :] = glu
    glu_buf[0:A_BACK, :] = _history_rows(glu[t - A_BACK:], glu_carry[...])
    glu_carry[...] = glu[t - A_BACK:]

    cv = zcol(3) * zcol(4)
    cv_buf[C3_BACK:C3_BACK + t, :] = cv
    cv_buf[0:C3_BACK, :] = _history_rows(cv[t - C3_BACK:], cv_carry[...])
    cv_carry[...] = cv[t - C3_BACK:]
    conv_b = None
    for k in range(FFN_CONV):
        term = cb_ref[k:k + 1, :] * cv_buf[k * SUBLANES:k * SUBLANES + t, :]
        conv_b = term if conv_b is None else conv_b + term
    ab_buf[:, A_WIDTH:] = (zcol(2) * conv_b).astype(jnp.bfloat16)

    def rows_body(i, carry):
        t0 = pl.multiple_of(i * A_ROWS, A_ROWS)
        cols = []
        for c in range(A_WIDTH // LANES):
            lanes = slice(c * LANES, (c + 1) * LANES)
            acc = None
            for k in range(A_CONV):
                term = ca_ref[k:k + 1, lanes] * glu_buf[pl.ds(t0 + k * SUBLANES, A_ROWS), lanes]
                acc = term if acc is None else acc + term
            cols.append(acc)
        a = jnp.concatenate(cols, axis=1)
        mu = jnp.mean(a, axis=-1, keepdims=True)
        ac = a - mu
        var = jnp.mean(ac * ac, axis=-1, keepdims=True)
        y = ac * lax.rsqrt(var + LN_EPS) * lng_ref[...] + lnb_ref[...]
        ab_buf[pl.ds(t0, A_ROWS), 0:A_WIDTH] = (y * _sigmoid(y)).astype(jnp.bfloat16)
        return carry

    lax.fori_loop(0, t // A_ROWS, rows_body, 0)

    o_ref[0] = x + _dot(ab_buf[...], w_out_ref[...])


def _pool_kernel(x_ref, g_ref, wp_ref, ps_ref, o_ref, carry):
    t = SEQ_TILE
    s = pl.program_id(1)

    @pl.when(s == 0)
    def _():
        carry[...] = jnp.zeros_like(carry)

    x = x_ref[0]
    h = _rms(x, g_ref[...])
    row = lax.broadcasted_iota(jnp.int32, (t, POOL_GROUP), 0)
    pos = s * t + (row & (SUBLANES - 1)) * SEG + (row >> 3)
    cur = h
    outs = []
    for lvl, w in enumerate(POOL_WINDOWS):
        lo = lvl * POOL_GROUP
        back = (w // 2) * SUBLANES
        tail = cur[t - back:]
        hist = _history_rows(tail, carry[lvl, 0:back, lo:])
        carry[lvl, 0:back, lo:] = tail
        nxt = cur + jnp.concatenate([hist, cur[:t - back]], axis=0)
        cnt = jnp.minimum(pos + 1, w).astype(jnp.float32)
        p = nxt[:, :POOL_GROUP] / cnt - h[:, lo:lo + POOL_GROUP]
        outs.append(_dot(p.astype(jnp.bfloat16), wp_ref[lvl]))
        cur = nxt[:, POOL_GROUP:]
    o_ref[0] = x + jnp.concatenate(outs, axis=1) * ps_ref[...]


def _ffn_kernel(x_ref, g_ref, w_up_ref, cw_ref, w_down_ref, gf_ref, o_ref, hb, acc_ref, carry,
                u0, u1, a0, a1, *, last):
    t = SEQ_TILE
    n = N_FF_CHUNKS
    ubuf = (u0, u1)
    act_buf = (a0, a1)

    @pl.when(pl.program_id(1) == 0)
    def _():
        carry[...] = jnp.zeros_like(carry)

    x = x_ref[0]
    hb[...] = _rms(x, g_ref[...]).astype(jnp.bfloat16)
    acc_ref[...] = x

    def up(j, slot):
        for half in range(2):
            idx = half * n + j
            u = _dot(hb[...], w_up_ref[idx])
            ubuf[slot][half, C3_BACK:C3_BACK + t, :] = u
            ubuf[slot][half, 0:C3_BACK, :] = _history_rows(u[t - C3_BACK:], carry[idx])
            carry[idx] = u[t - C3_BACK:]

    def act(j, slot):
        for rb in range(t // FF_ROWS):
            convs = []
            for half in range(2):
                w = cw_ref[half * n + j]
                conv = None
                for k in range(FFN_CONV):
                    r0 = rb * FF_ROWS + k * SUBLANES
                    term = w[k:k + 1] * ubuf[slot][half, r0:r0 + FF_ROWS, :]
                    conv = term if conv is None else conv + term
                convs.append(conv)
            gate, val = convs
            act_buf[slot][rb * FF_ROWS:(rb + 1) * FF_ROWS, :] = (
                gate * _sigmoid(gate) * val).astype(jnp.bfloat16)

    def down(j, slot):
        acc_ref[...] += _dot(act_buf[slot][...], w_down_ref[j])

    up(0, 0)
    for j in range(n):
        if j + 1 < n:
            up(j + 1, (j + 1) % 2)
        act(j, j % 2)
        down(j, j % 2)

    if last:
        acc_ref[...] = _rms(acc_ref[...], gf_ref[...])
        for b in range(SEG):
            o_ref[0, 0, :, b, :] = acc_ref[b * SUBLANES:(b + 1) * SUBLANES, :]
    else:
        o_ref[0] = acc_ref[...]


def _const_spec(shape):
    return pl.BlockSpec(shape, lambda b, s: (0,) * len(shape), pipeline_mode=pl.Buffered(1))


def _tile_spec(token_order):
    if token_order:
        return pl.BlockSpec((1, 1, SUBLANES, SEG, D_MODEL), lambda b, s: (b, s, 0, 0, 0))
    return pl.BlockSpec((1, SEQ_TILE, D_MODEL), lambda b, s: (b, s, 0))


def _call(body, x, consts, scratch, name, in_token_order=False, out_token_order=False):
    batch, seq, d = x.shape
    tiles = seq // SEQ_TILE
    token_shape = (batch, tiles, SUBLANES, SEG, d)
    if in_token_order:
        x = x.reshape(token_shape)
    out = pl.pallas_call(
        body,
        grid=(batch, tiles),
        in_specs=[_tile_spec(in_token_order)] + [_const_spec(c.shape) for c in consts],
        out_specs=_tile_spec(out_token_order),
        out_shape=jax.ShapeDtypeStruct(token_shape if out_token_order else (batch, seq, d), x.dtype),
        scratch_shapes=scratch,
        compiler_params=pltpu.CompilerParams(
            dimension_semantics=("arbitrary", "arbitrary"), vmem_limit_bytes=VMEM_LIMIT),
        name=name,
    )(x, *consts)
    return out.reshape(batch, seq, d)


def _mixer0(x, g, w_in, conv_a, ln_g, ln_b, conv_b, w_out):
    consts = [g.reshape(1, -1), w_in.astype(jnp.bfloat16), conv_a, ln_g.reshape(1, -1),
              ln_b.reshape(1, -1), conv_b, w_out.astype(jnp.bfloat16)]
    scratch = [pltpu.VMEM((SEQ_TILE, D_MODEL), jnp.float32),
               pltpu.VMEM((A_BACK + SEQ_TILE, A_WIDTH), jnp.float32),
               pltpu.VMEM((A_BACK, A_WIDTH), jnp.float32),
               pltpu.VMEM((C3_BACK + SEQ_TILE, B_WIDTH), jnp.float32),
               pltpu.VMEM((C3_BACK, B_WIDTH), jnp.float32),
               pltpu.VMEM((SEQ_TILE, D_MODEL), jnp.bfloat16)]
    return _call(_mixer0_kernel, x, consts, scratch, "mixer0", in_token_order=True)


def _pool(x, g, w_pool, pool_scale):
    consts = [g.reshape(1, -1), w_pool.astype(jnp.bfloat16), pool_scale.reshape(1, -1)]
    scratch = [pltpu.VMEM((len(POOL_WINDOWS), POOL_BACK, D_MODEL), jnp.float32)]
    return _call(_pool_kernel, x, consts, scratch, "pool_mixer")


def _ffn(x, g, w_up, conv_w, w_down, g_final, last, name):
    w_up_c = w_up.astype(jnp.bfloat16).reshape(D_MODEL, 2 * N_FF_CHUNKS, FF_CHUNK).transpose(1, 0, 2)
    conv_c = conv_w.reshape(FFN_CONV, 2 * N_FF_CHUNKS, FF_CHUNK).transpose(1, 0, 2)
    w_down_c = w_down.astype(jnp.bfloat16).reshape(N_FF_CHUNKS, FF_CHUNK, D_MODEL)
    consts = [g.reshape(1, -1), w_up_c, conv_c, w_down_c, g_final.reshape(1, -1)]
    scratch = [pltpu.VMEM((SEQ_TILE, D_MODEL), jnp.bfloat16),
               pltpu.VMEM((SEQ_TILE, D_MODEL), jnp.float32),
               pltpu.VMEM((2 * N_FF_CHUNKS, C3_BACK, FF_CHUNK), jnp.float32),
               pltpu.VMEM((2, C3_BACK + SEQ_TILE, FF_CHUNK), jnp.float32),
               pltpu.VMEM((2, C3_BACK + SEQ_TILE, FF_CHUNK), jnp.float32),
               pltpu.VMEM((SEQ_TILE, FF_CHUNK), jnp.bfloat16),
               pltpu.VMEM((SEQ_TILE, FF_CHUNK), jnp.bfloat16)]
    return _call(functools.partial(_ffn_kernel, last=last), x, consts, scratch, name,
                 out_token_order=last)


def kernel(x, norm_mix_even, w_in, conv_a, ln_a_g, ln_a_b, conv_b, w_out, norm_mix_odd, w_pool, pool_scale, norm_ffn, w_up, conv_ffn_w, w_down, norm_final):
    x = _mixer0(x, norm_mix_even[0], w_in[0], conv_a[0], ln_a_g[0], ln_a_b[0], conv_b[0], w_out[0])
    x = _ffn(x, norm_ffn[0], w_up[0], conv_ffn_w[0], w_down[0], norm_final, False, "ffn0")
    x = _pool(x, norm_mix_odd[0], w_pool[0], pool_scale[0])
    x = _ffn(x, norm_ffn[1], w_up[1], conv_ffn_w[1], w_down[1], norm_final, True, "ffn1")
    return x
```

```python
import functools

import jax
import jax.numpy as jnp
from jax import lax
from jax.experimental import pallas as pl
from jax.experimental.pallas import tpu as pltpu

D_MODEL = 1024
A_WIDTH = 512
B_WIDTH = 512
A_CONV = 31
POOL_WINDOWS = (2, 4, 8, 16)
POOL_GROUP = 256
D_FF = 2816
FFN_CONV = 3
RMS_EPS = 1e-6
LN_EPS = 1e-5

SUBLANES = 8
LANES = 128
SEQ_TILE = 512
SEG = SEQ_TILE // SUBLANES
FF_CHUNK = 256
N_FF_CHUNKS = D_FF // FF_CHUNK
FF_PLANES = FF_CHUNK // LANES
FF_ROWS = 64
A_ROWS = 64
A_PARTIALS = 4
A_BACK = (A_CONV - 1) * SUBLANES
C3_BACK = (FFN_CONV - 1) * SUBLANES
POOL_BACK = (POOL_WINDOWS[-1] // 2) * SUBLANES
VMEM_LIMIT = 58 * 1024 * 1024

_dot = functools.partial(jnp.dot, preferred_element_type=jnp.float32)


def _rms(x, g):
    ms = jnp.mean(x * x, axis=-1, keepdims=True)
    return x * lax.rsqrt(ms + RMS_EPS) * g


def _sigmoid(x):
    return 1.0 / (1.0 + jnp.exp(-x))


def _silu(x):
    hx = 0.5 * x
    return hx + hx * jnp.tanh(hx)


def _history_rows(cur_tail, prev_tail):
    rows = lax.broadcasted_iota(jnp.int32, (SUBLANES, cur_tail.shape[1]), 0)
    groups = []
    for i in range(cur_tail.shape[0] // SUBLANES):
        sl = slice(i * SUBLANES, (i + 1) * SUBLANES)
        groups.append(jnp.where(rows == 0, pltpu.roll(prev_tail[sl], 1, axis=0),
                                pltpu.roll(cur_tail[sl], 1, axis=0)))
    return groups[0] if len(groups) == 1 else jnp.concatenate(groups, axis=0)


def _stage(buf, first_plane, v, back, prev_tail):
    rows = v.shape[0]
    hist = _history_rows(v[rows - back:], prev_tail)
    for c in range(v.shape[1] // LANES):
        lanes = slice(c * LANES, (c + 1) * LANES)
        buf[first_plane + c, 0:back, :] = hist[:, lanes]
        buf[first_plane + c, back:back + rows, :] = v[:, lanes]


def _conv_mixer(x, g_ref, w_in_ref, ca_ref, lng_ref, lnb_ref, cb_ref, w_out_ref,
                glu_buf, glu_carry, a_buf, cv_buf, cv_carry, ab_buf):
    t = SEQ_TILE
    h = _rms(x, g_ref[...]).astype(jnp.bfloat16)

    def zcol(i):
        return _dot(h, w_in_ref[:, i * A_WIDTH:(i + 1) * A_WIDTH])

    glu = zcol(0) * _sigmoid(zcol(1))
    _stage(glu_buf, 0, glu, A_BACK, glu_carry[...])
    glu_carry[...] = glu[t - A_BACK:]

    for c in range(A_WIDTH // LANES):
        lanes = slice(c * LANES, (c + 1) * LANES)

        def conv_body(i, carry, c=c, lanes=lanes):
            t0 = pl.multiple_of(i * A_ROWS, A_ROWS)
            parts = [None] * A_PARTIALS
            for k in range(A_CONV):
                term = ca_ref[k:k + 1, lanes] * glu_buf[c, pl.ds(t0 + k * SUBLANES, A_ROWS), :]
                p = k % A_PARTIALS
                parts[p] = term if parts[p] is None else parts[p] + term
            while len(parts) > 1:
                parts = [parts[i] + parts[i + 1] for i in range(0, len(parts), 2)]
            a_buf[pl.ds(t0, A_ROWS), lanes] = parts[0]
            return carry

        lax.fori_loop(0, t // A_ROWS, conv_body, 0)

    cv = zcol(3) * zcol(4)
    _stage(cv_buf, 0, cv, C3_BACK, cv_carry[...])
    cv_carry[...] = cv[t - C3_BACK:]
    b_gate = zcol(2)
    for c in range(B_WIDTH // LANES):
        lanes = slice(c * LANES, (c + 1) * LANES)
        conv_b = None
        for k in range(FFN_CONV):
            term = cb_ref[k:k + 1, lanes] * cv_buf[c, k * SUBLANES:k * SUBLANES + t, :]
            conv_b = term if conv_b is None else conv_b + term
        ab_buf[:, A_WIDTH + c * LANES:A_WIDTH + (c + 1) * LANES] = (
            b_gate[:, lanes] * conv_b).astype(jnp.bfloat16)

    for i in range(t // A_ROWS):
        rows = slice(i * A_ROWS, (i + 1) * A_ROWS)
        a = a_buf[rows, :]
        mu = jnp.mean(a, axis=-1, keepdims=True)
        ac = a - mu
        var = jnp.mean(ac * ac, axis=-1, keepdims=True)
        y = ac * lax.rsqrt(var + LN_EPS) * lng_ref[...] + lnb_ref[...]
        ab_buf[rows, 0:A_WIDTH] = _silu(y).astype(jnp.bfloat16)

    return x + _dot(ab_buf[...], w_out_ref[...])


def _pool_mixer(x, g_ref, wp_ref, ps_ref, carry):
    t = SEQ_TILE
    h = _rms(x, g_ref[...])
    row = lax.broadcasted_iota(jnp.int32, (t, LANES), 0)
    pos = pl.program_id(1) * t + (row & (SUBLANES - 1)) * SEG + (row >> 3)
    cur = h
    outs = []
    for lvl, w in enumerate(POOL_WINDOWS):
        lo = lvl * POOL_GROUP
        back = (w // 2) * SUBLANES
        tail = cur[t - back:]
        hist = _history_rows(tail, carry[lvl, 0:back, lo:])
        carry[lvl, 0:back, lo:] = tail
        nxt = cur + jnp.concatenate([hist, cur[:t - back]], axis=0)
        inv = 1.0 / jnp.minimum(pos + 1, w).astype(jnp.float32)
        inv = jnp.concatenate([inv] * (POOL_GROUP // LANES), axis=1)
        p = nxt[:, :POOL_GROUP] * inv - h[:, lo:lo + POOL_GROUP]
        outs.append(_dot(p.astype(jnp.bfloat16), wp_ref[lvl]))
        cur = nxt[:, POOL_GROUP:]
    return x + jnp.concatenate(outs, axis=1) * ps_ref[...]


def _conv_ffn(g_ref, w_up_ref, cw_ref, w_down_ref, hb, acc_ref, carry, ubuf, act_buf):
    t = SEQ_TILE
    n = N_FF_CHUNKS
    hb[...] = _rms(acc_ref[...], g_ref[...]).astype(jnp.bfloat16)

    def up(j, slot):
        for half in range(2):
            idx = half * n + j
            u = _dot(hb[...], w_up_ref[idx])
            _stage(ubuf[slot], half * FF_PLANES, u, C3_BACK, carry[idx])
            carry[idx] = u[t - C3_BACK:]

    def act(j, slot):
        for rb in range(t // FF_ROWS):
            for c in range(FF_PLANES):
                lanes = slice(c * LANES, (c + 1) * LANES)
                convs = []
                for half in range(2):
                    w = cw_ref[half * n + j]
                    conv = None
                    for k in range(FFN_CONV):
                        r0 = rb * FF_ROWS + k * SUBLANES
                        term = w[k:k + 1, lanes] * ubuf[slot][half * FF_PLANES + c, r0:r0 + FF_ROWS, :]
                        conv = term if conv is None else conv + term
                    convs.append(conv)
                gate, val = convs
                act_buf[slot][rb * FF_ROWS:(rb + 1) * FF_ROWS, lanes] = (
                    _silu(gate) * val).astype(jnp.bfloat16)

    def down(j, slot):
        acc_ref[...] += _dot(act_buf[slot][...], w_down_ref[j])

    up(0, 0)
    for j in range(n):
        if j + 1 < n:
            up(j + 1, (j + 1) % 2)
        act(j, j % 2)
        down(j, j % 2)


def _layer0_kernel(x_ref, gm_ref, w_in_ref, ca_ref, lng_ref, lnb_ref, cb_ref, w_out_ref,
                   gf_ref, w_up_ref, cw_ref, w_down_ref, o_ref,
                   xp, glu_buf, glu_carry, a_buf, cv_buf, cv_carry, ab_buf,
                   hb, acc_ref, ffn_carry, u0, u1, a0, a1):
    @pl.when(pl.program_id(1) == 0)
    def _():
        glu_carry[...] = jnp.zeros_like(glu_carry)
        cv_carry[...] = jnp.zeros_like(cv_carry)
        ffn_carry[...] = jnp.zeros_like(ffn_carry)

    for b in range(SEG):
        xp[b * SUBLANES:(b + 1) * SUBLANES, :] = x_ref[0, 0, :, b, :]
    acc_ref[...] = _conv_mixer(xp[...], gm_ref, w_in_ref, ca_ref, lng_ref, lnb_ref, cb_ref, w_out_ref,
                               glu_buf, glu_carry, a_buf, cv_buf, cv_carry, ab_buf)
    _conv_ffn(gf_ref, w_up_ref, cw_ref, w_down_ref, hb, acc_ref, ffn_carry, (u0, u1), (a0, a1))
    o_ref[0] = acc_ref[...]


def _layer1_kernel(x_ref, gm_ref, wp_ref, ps_ref, gf_ref, w_up_ref, cw_ref, w_down_ref, gl_ref, o_ref,
                   pool_carry, hb, acc_ref, ffn_carry, u0, u1, a0, a1):
    @pl.when(pl.program_id(1) == 0)
    def _():
        pool_carry[...] = jnp.zeros_like(pool_carry)
        ffn_carry[...] = jnp.zeros_like(ffn_carry)

    acc_ref[...] = _pool_mixer(x_ref[0], gm_ref, wp_ref, ps_ref, pool_carry)
    _conv_ffn(gf_ref, w_up_ref, cw_ref, w_down_ref, hb, acc_ref, ffn_carry, (u0, u1), (a0, a1))
    acc_ref[...] = _rms(acc_ref[...], gl_ref[...])
    for b in range(SEG):
        o_ref[0, 0, :, b, :] = acc_ref[b * SUBLANES:(b + 1) * SUBLANES, :]


def _const_spec(shape):
    return pl.BlockSpec(shape, lambda b, s: (0,) * len(shape), pipeline_mode=pl.Buffered(1))


def _tile_spec(token_order):
    if token_order:
        return pl.BlockSpec((1, 1, SUBLANES, SEG, D_MODEL), lambda b, s: (b, s, 0, 0, 0))
    return pl.BlockSpec((1, SEQ_TILE, D_MODEL), lambda b, s: (b, s, 0))


def _call(body, x, consts, scratch, name, in_token_order=False, out_token_order=False):
    batch, seq, d = x.shape
    tiles = seq // SEQ_TILE
    token_shape = (batch, tiles, SUBLANES, SEG, d)
    if in_token_order:
        x = x.reshape(token_shape)
    out = pl.pallas_call(
        body,
        grid=(batch, tiles),
        in_specs=[_tile_spec(in_token_order)] + [_const_spec(c.shape) for c in consts],
        out_specs=_tile_spec(out_token_order),
        out_shape=jax.ShapeDtypeStruct(token_shape if out_token_order else (batch, seq, d), x.dtype),
        scratch_shapes=scratch,
        compiler_params=pltpu.CompilerParams(
            dimension_semantics=("arbitrary", "arbitrary"), vmem_limit_bytes=VMEM_LIMIT),
        name=name,
    )(x, *consts)
    return out.reshape(batch, seq, d)


def _ffn_consts(g, w_up, conv_w, w_down):
    w_up_c = w_up.astype(jnp.bfloat16).reshape(D_MODEL, 2 * N_FF_CHUNKS, FF_CHUNK).transpose(1, 0, 2)
    conv_c = conv_w.reshape(FFN_CONV, 2 * N_FF_CHUNKS, FF_CHUNK).transpose(1, 0, 2)
    w_down_c = w_down.astype(jnp.bfloat16).reshape(N_FF_CHUNKS, FF_CHUNK, D_MODEL)
    return [g.reshape(1, -1), w_up_c, conv_c, w_down_c]


def _ffn_scratch():
    return [pltpu.VMEM((SEQ_TILE, D_MODEL), jnp.bfloat16),
            pltpu.VMEM((SEQ_TILE, D_MODEL), jnp.float32),
            pltpu.VMEM((2 * N_FF_CHUNKS, C3_BACK, FF_CHUNK), jnp.float32),
            pltpu.VMEM((2 * FF_PLANES, C3_BACK + SEQ_TILE, LANES), jnp.float32),
            pltpu.VMEM((2 * FF_PLANES, C3_BACK + SEQ_TILE, LANES), jnp.float32),
            pltpu.VMEM((SEQ_TILE, FF_CHUNK), jnp.bfloat16),
            pltpu.VMEM((SEQ_TILE, FF_CHUNK), jnp.bfloat16)]


def kernel(x, norm_mix_even, w_in, conv_a, ln_a_g, ln_a_b, conv_b, w_out, norm_mix_odd, w_pool, pool_scale, norm_ffn, w_up, conv_ffn_w, w_down, norm_final):
    consts0 = [norm_mix_even[0].reshape(1, -1), w_in[0].astype(jnp.bfloat16), conv_a[0],
               ln_a_g[0].reshape(1, -1), ln_a_b[0].reshape(1, -1), conv_b[0],
               w_out[0].astype(jnp.bfloat16)] + _ffn_consts(norm_ffn[0], w_up[0], conv_ffn_w[0], w_down[0])
    scratch0 = [pltpu.VMEM((SEQ_TILE, D_MODEL), jnp.float32),
                pltpu.VMEM((A_WIDTH // LANES, A_BACK + SEQ_TILE, LANES), jnp.float32),
                pltpu.VMEM((A_BACK, A_WIDTH), jnp.float32),
                pltpu.VMEM((SEQ_TILE, A_WIDTH), jnp.float32),
                pltpu.VMEM((B_WIDTH // LANES, C3_BACK + SEQ_TILE, LANES), jnp.float32),
                pltpu.VMEM((C3_BACK, B_WIDTH), jnp.float32),
                pltpu.VMEM((SEQ_TILE, D_MODEL), jnp.bfloat16)] + _ffn_scratch()
    x = _call(_layer0_kernel, x, consts0, scratch0, "layer0", in_token_order=True)

    consts1 = [norm_mix_odd[0].reshape(1, -1), w_pool[0].astype(jnp.bfloat16),
               pool_scale[0].reshape(1, -1)] + _ffn_consts(norm_ffn[1], w_up[1], conv_ffn_w[1], w_down[1]) + [
               norm_final.reshape(1, -1)]
    scratch1 = [pltpu.VMEM((len(POOL_WINDOWS), POOL_BACK, D_MODEL), jnp.float32)] + _ffn_scratch()
    return _call(_layer1_kernel, x, consts1, scratch1, "layer1", out_token_order=True)
```

```python
import functools

import jax
import jax.numpy as jnp
from jax import lax
from jax.experimental import pallas as pl
from jax.experimental.pallas import tpu as pltpu

D_MODEL = 1024
A_WIDTH = 512
B_WIDTH = 512
A_CONV = 31
POOL_WINDOWS = (2, 4, 8, 16)
POOL_GROUP = 256
D_FF = 2816
FFN_CONV = 3
RMS_EPS = 1e-6
LN_EPS = 1e-5

SUBLANES = 8
LANES = 128
SEQ_TILE = 512
SEG = SEQ_TILE // SUBLANES
FF_CHUNK = 256
N_FF_CHUNKS = D_FF // FF_CHUNK
FF_PLANES = FF_CHUNK // LANES
FF_ROWS = 64
A_ROWS = 64
A_PARTIALS = 4
A_BACK = (A_CONV - 1) * SUBLANES
C3_BACK = (FFN_CONV - 1) * SUBLANES
POOL_BACK = (POOL_WINDOWS[-1] // 2) * SUBLANES
VMEM_LIMIT = 58 * 1024 * 1024

_dot = functools.partial(jnp.dot, preferred_element_type=jnp.float32)


def _rms(x, g):
    ms = jnp.mean(x * x, axis=-1, keepdims=True)
    return x * lax.rsqrt(ms + RMS_EPS) * g


def _sigmoid(x):
    return 1.0 / (1.0 + jnp.exp(-x))


def _silu(x):
    hx = 0.5 * x
    return hx + hx * jnp.tanh(hx)


def _history_rows(cur_tail, prev_tail):
    rows = lax.broadcasted_iota(jnp.int32, (SUBLANES, cur_tail.shape[1]), 0)
    groups = []
    for i in range(cur_tail.shape[0] // SUBLANES):
        sl = slice(i * SUBLANES, (i + 1) * SUBLANES)
        groups.append(jnp.where(rows == 0, pltpu.roll(prev_tail[sl], 1, axis=0),
                                pltpu.roll(cur_tail[sl], 1, axis=0)))
    return groups[0] if len(groups) == 1 else jnp.concatenate(groups, axis=0)


def _stage(buf, first_plane, v, back, prev_tail):
    rows = v.shape[0]
    hist = _history_rows(v[rows - back:], prev_tail)
    for c in range(v.shape[1] // LANES):
        lanes = slice(c * LANES, (c + 1) * LANES)
        buf[first_plane + c, 0:back, :] = hist[:, lanes]
        buf[first_plane + c, back:back + rows, :] = v[:, lanes]


def _conv_mixer(x, g_ref, w_in_ref, ca_ref, lng_ref, lnb_ref, cb_ref, w_out_ref,
                glu_buf, glu_carry, a_buf, cv_buf, cv_carry, ab_buf):
    t = SEQ_TILE
    h = _rms(x, g_ref[...]).astype(jnp.bfloat16)

    def zcol(i):
        return _dot(h, w_in_ref[:, i * A_WIDTH:(i + 1) * A_WIDTH])

    glu = zcol(0) * _sigmoid(zcol(1))
    _stage(glu_buf, 0, glu, A_BACK, glu_carry[...])
    glu_carry[...] = glu[t - A_BACK:]

    for c in range(A_WIDTH // LANES):
        lanes = slice(c * LANES, (c + 1) * LANES)

        def conv_body(i, carry, c=c, lanes=lanes):
            t0 = pl.multiple_of(i * A_ROWS, A_ROWS)
            parts = [None] * A_PARTIALS
            for k in range(A_CONV):
                term = ca_ref[k:k + 1, lanes] * glu_buf[c, pl.ds(t0 + k * SUBLANES, A_ROWS), :]
                p = k % A_PARTIALS
                parts[p] = term if parts[p] is None else parts[p] + term
            while len(parts) > 1:
                parts = [parts[q] + parts[q + 1] for q in range(0, len(parts), 2)]
            a_buf[pl.ds(t0, A_ROWS), lanes] = parts[0]
            return carry

        lax.fori_loop(0, t // A_ROWS, conv_body, 0)

    cv = zcol(3) * zcol(4)
    _stage(cv_buf, 0, cv, C3_BACK, cv_carry[...])
    cv_carry[...] = cv[t - C3_BACK:]
    b_gate = zcol(2)
    for c in range(B_WIDTH // LANES):
        lanes = slice(c * LANES, (c + 1) * LANES)
        conv_b = None
        for k in range(FFN_CONV):
            term = cb_ref[k:k + 1, lanes] * cv_buf[c, k * SUBLANES:k * SUBLANES + t, :]
            conv_b = term if conv_b is None else conv_b + term
        ab_buf[:, A_WIDTH + c * LANES:A_WIDTH + (c + 1) * LANES] = (
            b_gate[:, lanes] * conv_b).astype(jnp.bfloat16)

    for i in range(t // A_ROWS):
        rows = slice(i * A_ROWS, (i + 1) * A_ROWS)
        a = a_buf[rows, :]
        mu = jnp.mean(a, axis=-1, keepdims=True)
        ac = a - mu
        var = jnp.mean(ac * ac, axis=-1, keepdims=True)
        y = ac * lax.rsqrt(var + LN_EPS) * lng_ref[...] + lnb_ref[...]
        ab_buf[rows, 0:A_WIDTH] = _silu(y).astype(jnp.bfloat16)

    return x + _dot(ab_buf[...], w_out_ref[...])


def _pool_mixer(x, g_ref, wp_ref, ps_ref, carry):
    t = SEQ_TILE
    h = _rms(x, g_ref[...])
    row = lax.broadcasted_iota(jnp.int32, (t, LANES), 0)
    pos = pl.program_id(1) * t + (row & (SUBLANES - 1)) * SEG + (row >> 3)
    cur = h
    outs = []
    for lvl, w in enumerate(POOL_WINDOWS):
        lo = lvl * POOL_GROUP
        back = (w // 2) * SUBLANES
        tail = cur[t - back:]
        hist = _history_rows(tail, carry[lvl, 0:back, lo:])
        carry[lvl, 0:back, lo:] = tail
        nxt = cur + jnp.concatenate([hist, cur[:t - back]], axis=0)
        inv = 1.0 / jnp.minimum(pos + 1, w).astype(jnp.float32)
        inv = jnp.concatenate([inv] * (POOL_GROUP // LANES), axis=1)
        p = nxt[:, :POOL_GROUP] * inv - h[:, lo:lo + POOL_GROUP]
        outs.append(_dot(p.astype(jnp.bfloat16), wp_ref[lvl]))
        cur = nxt[:, POOL_GROUP:]
    return x + jnp.concatenate(outs, axis=1) * ps_ref[...]


def _conv_ffn(g_ref, w_up_ref, cw_ref, w_down_ref, hb, acc_ref, carry, ubuf, act_buf):
    t = SEQ_TILE
    n = N_FF_CHUNKS
    hb[...] = _rms(acc_ref[...], g_ref[...]).astype(jnp.bfloat16)

    def cols(half, j):
        return half * D_FF + j * FF_CHUNK

    def up(j, slot):
        for half in range(2):
            idx = half * n + j
            u = _dot(hb[...], w_up_ref[:, cols(half, j):cols(half, j) + FF_CHUNK])
            _stage(ubuf[slot], half * FF_PLANES, u, C3_BACK, carry[idx])
            carry[idx] = u[t - C3_BACK:]

    def act(j, slot):
        for rb in range(t // FF_ROWS):
            for c in range(FF_PLANES):
                lanes = slice(c * LANES, (c + 1) * LANES)
                convs = []
                for half in range(2):
                    c0 = cols(half, j) + c * LANES
                    conv = None
                    for k in range(FFN_CONV):
                        r0 = rb * FF_ROWS + k * SUBLANES
                        term = cw_ref[k:k + 1, c0:c0 + LANES] * ubuf[slot][half * FF_PLANES + c, r0:r0 + FF_ROWS, :]
                        conv = term if conv is None else conv + term
                    convs.append(conv)
                gate, val = convs
                act_buf[slot][rb * FF_ROWS:(rb + 1) * FF_ROWS, lanes] = (
                    _silu(gate) * val).astype(jnp.bfloat16)

    def down(j, slot):
        acc_ref[...] += _dot(act_buf[slot][...], w_down_ref[j * FF_CHUNK:(j + 1) * FF_CHUNK, :])

    up(0, 0)
    for j in range(n):
        if j + 1 < n:
            up(j + 1, (j + 1) % 2)
        act(j, j % 2)
        down(j, j % 2)


def _layer0_kernel(x_ref, gm_ref, w_in_ref, ca_ref, lng_ref, lnb_ref, cb_ref, w_out_ref,
                   gf_ref, w_up_ref, cw_ref, w_down_ref, o_ref,
                   xp, glu_buf, glu_carry, a_buf, cv_buf, cv_carry, ab_buf,
                   hb, acc_ref, ffn_carry, u0, u1, a0, a1):
    @pl.when(pl.program_id(1) == 0)
    def _():
        glu_carry[...] = jnp.zeros_like(glu_carry)
        cv_carry[...] = jnp.zeros_like(cv_carry)
        ffn_carry[...] = jnp.zeros_like(ffn_carry)

    for b in range(SEG):
        xp[b * SUBLANES:(b + 1) * SUBLANES, :] = x_ref[0, 0, :, b, :]
    acc_ref[...] = _conv_mixer(xp[...], gm_ref, w_in_ref, ca_ref, lng_ref, lnb_ref, cb_ref, w_out_ref,
                               glu_buf, glu_carry, a_buf, cv_buf, cv_carry, ab_buf)
    _conv_ffn(gf_ref, w_up_ref, cw_ref, w_down_ref, hb, acc_ref, ffn_carry, (u0, u1), (a0, a1))
    o_ref[0] = acc_ref[...]


def _layer1_kernel(x_ref, gm_ref, wp_ref, ps_ref, gf_ref, w_up_ref, cw_ref, w_down_ref, gl_ref, o_ref,
                   pool_carry, hb, acc_ref, ffn_carry, u0, u1, a0, a1):
    @pl.when(pl.program_id(1) == 0)
    def _():
        pool_carry[...] = jnp.zeros_like(pool_carry)
        ffn_carry[...] = jnp.zeros_like(ffn_carry)

    acc_ref[...] = _pool_mixer(x_ref[0], gm_ref, wp_ref, ps_ref, pool_carry)
    _conv_ffn(gf_ref, w_up_ref, cw_ref, w_down_ref, hb, acc_ref, ffn_carry, (u0, u1), (a0, a1))
    acc_ref[...] = _rms(acc_ref[...], gl_ref[...])
    for b in range(SEG):
        o_ref[0, 0, :, b, :] = acc_ref[b * SUBLANES:(b + 1) * SUBLANES, :]


def _param_spec(stacked, layer):
    rest = stacked.shape[1:]
    return pl.BlockSpec((None,) + rest, lambda b, s: (layer,) + (0,) * len(rest),
                        pipeline_mode=pl.Buffered(1))


def _tile_spec(token_order):
    if token_order:
        return pl.BlockSpec((1, 1, SUBLANES, SEG, D_MODEL), lambda b, s: (b, s, 0, 0, 0))
    return pl.BlockSpec((1, SEQ_TILE, D_MODEL), lambda b, s: (b, s, 0))


def _call(body, x, params, scratch, name, in_token_order=False, out_token_order=False):
    batch, seq, d = x.shape
    tiles = seq // SEQ_TILE
    token_shape = (batch, tiles, SUBLANES, SEG, d)
    if in_token_order:
        x = x.reshape(token_shape)
    out = pl.pallas_call(
        body,
        grid=(batch, tiles),
        in_specs=[_tile_spec(in_token_order)] + [_param_spec(p, layer) for p, layer in params],
        out_specs=_tile_spec(out_token_order),
        out_shape=jax.ShapeDtypeStruct(token_shape if out_token_order else (batch, seq, d), x.dtype),
        scratch_shapes=scratch,
        compiler_params=pltpu.CompilerParams(
            dimension_semantics=("arbitrary", "arbitrary"), vmem_limit_bytes=VMEM_LIMIT),
        name=name,
    )(x, *[p for p, _ in params])
    return out.reshape(batch, seq, d)


def _ffn_scratch():
    return [pltpu.VMEM((SEQ_TILE, D_MODEL), jnp.bfloat16),
            pltpu.VMEM((SEQ_TILE, D_MODEL), jnp.float32),
            pltpu.VMEM((2 * N_FF_CHUNKS, C3_BACK, FF_CHUNK), jnp.float32),
            pltpu.VMEM((2 * FF_PLANES, C3_BACK + SEQ_TILE, LANES), jnp.float32),
            pltpu.VMEM((2 * FF_PLANES, C3_BACK + SEQ_TILE, LANES), jnp.float32),
            pltpu.VMEM((SEQ_TILE, FF_CHUNK), jnp.bfloat16),
            pltpu.VMEM((SEQ_TILE, FF_CHUNK), jnp.bfloat16)]


def kernel(x, norm_mix_even, w_in, conv_a, ln_a_g, ln_a_b, conv_b, w_out, norm_mix_odd, w_pool, pool_scale, norm_ffn, w_up, conv_ffn_w, w_down, norm_final):
    def row(v):
        return v.reshape(v.shape[0], 1, v.shape[1])

    bf16 = lambda w: w.astype(jnp.bfloat16)
    norm_ffn_r, w_up_b, w_down_b = row(norm_ffn), bf16(w_up), bf16(w_down)

    def ffn_params(layer):
        return [(norm_ffn_r, layer), (w_up_b, layer), (conv_ffn_w, layer), (w_down_b, layer)]

    params0 = [(row(norm_mix_even), 0), (bf16(w_in), 0), (conv_a, 0), (row(ln_a_g), 0), (row(ln_a_b), 0),
               (conv_b, 0), (bf16(w_out), 0)] + ffn_params(0)
    scratch0 = [pltpu.VMEM((SEQ_TILE, D_MODEL), jnp.float32),
                pltpu.VMEM((A_WIDTH // LANES, A_BACK + SEQ_TILE, LANES), jnp.float32),
                pltpu.VMEM((A_BACK, A_WIDTH), jnp.float32),
                pltpu.VMEM((SEQ_TILE, A_WIDTH), jnp.float32),
                pltpu.VMEM((B_WIDTH // LANES, C3_BACK + SEQ_TILE, LANES), jnp.float32),
                pltpu.VMEM((C3_BACK, B_WIDTH), jnp.float32),
                pltpu.VMEM((SEQ_TILE, D_MODEL), jnp.bfloat16)] + _ffn_scratch()
    x = _call(_layer0_kernel, x, params0, scratch0, "layer0", in_token_order=True)

    params1 = [(row(norm_mix_odd), 0), (bf16(w_pool), 0), (row(pool_scale), 0)] + ffn_params(1) + [
        (norm_final.reshape(1, 1, -1), 0)]
    scratch1 = [pltpu.VMEM((len(POOL_WINDOWS), POOL_BACK, D_MODEL), jnp.float32)] + _ffn_scratch()
    return _call(_layer1_kernel, x, params1, scratch1, "layer1", out_token_order=True)
```

```python
import functools

import jax
import jax.numpy as jnp
from jax import lax
from jax.experimental import pallas as pl
from jax.experimental.pallas import tpu as pltpu

D_MODEL = 1024
A_WIDTH = 512
B_WIDTH = 512
A_CONV = 31
POOL_WINDOWS = (2, 4, 8, 16)
POOL_GROUP = 256
D_FF = 2816
FFN_CONV = 3
RMS_EPS = 1e-6
LN_EPS = 1e-5

SUBLANES = 8
SUBLANE_BITS = SUBLANES.bit_length() - 1
LANES = 128
SEQ_TILE = 512
SEG = SEQ_TILE // SUBLANES
FF_CHUNK = 256
N_FF_CHUNKS = D_FF // FF_CHUNK
FF_PLANES = FF_CHUNK // LANES
FF_ROWS = 64
A_ROWS = 64
A_PARTIALS = 4
A_BACK = (A_CONV - 1) * SUBLANES
C3_BACK = (FFN_CONV - 1) * SUBLANES
POOL_BACK = (POOL_WINDOWS[-1] // 2) * SUBLANES
VMEM_LIMIT = 58 * 1024 * 1024

_dot = functools.partial(jnp.dot, preferred_element_type=jnp.float32)


def _rms(x, g):
    ms = jnp.mean(x * x, axis=-1, keepdims=True)
    return x * lax.rsqrt(ms + RMS_EPS) * g


def _sigmoid(x):
    return 1.0 / (1.0 + jnp.exp(-x))


def _silu(x):
    hx = 0.5 * x
    return hx + hx * jnp.tanh(hx)


def _history_rows(cur_tail, prev_tail):
    rows = lax.broadcasted_iota(jnp.int32, (SUBLANES, cur_tail.shape[1]), 0)
    groups = []
    for i in range(cur_tail.shape[0] // SUBLANES):
        sl = slice(i * SUBLANES, (i + 1) * SUBLANES)
        groups.append(jnp.where(rows == 0, pltpu.roll(prev_tail[sl], 1, axis=0),
                                pltpu.roll(cur_tail[sl], 1, axis=0)))
    return groups[0] if len(groups) == 1 else jnp.concatenate(groups, axis=0)


def _stage(buf, first_plane, v, back, prev_tail):
    rows = v.shape[0]
    hist = _history_rows(v[rows - back:], prev_tail)
    for c in range(v.shape[1] // LANES):
        lanes = slice(c * LANES, (c + 1) * LANES)
        buf[first_plane + c, 0:back, :] = hist[:, lanes]
        buf[first_plane + c, back:back + rows, :] = v[:, lanes]


def _conv_mixer(x, g_ref, w_in_ref, ca_ref, lng_ref, lnb_ref, cb_ref, w_out_ref,
                glu_buf, glu_carry, a_buf, cv_buf, cv_carry, ab_buf):
    t = SEQ_TILE
    h = _rms(x, g_ref[...]).astype(jnp.bfloat16)

    def zcol(i):
        return _dot(h, w_in_ref[:, i * A_WIDTH:(i + 1) * A_WIDTH])

    glu = zcol(0) * _sigmoid(zcol(1))
    _stage(glu_buf, 0, glu, A_BACK, glu_carry[...])
    glu_carry[...] = glu[t - A_BACK:]

    for c in range(A_WIDTH // LANES):
        lanes = slice(c * LANES, (c + 1) * LANES)

        def conv_body(i, carry, c=c, lanes=lanes):
            t0 = pl.multiple_of(i * A_ROWS, A_ROWS)
            parts = [None] * A_PARTIALS
            for k in range(A_CONV):
                term = ca_ref[k:k + 1, lanes] * glu_buf[c, pl.ds(t0 + k * SUBLANES, A_ROWS), :]
                p = k % A_PARTIALS
                parts[p] = term if parts[p] is None else parts[p] + term
            while len(parts) > 1:
                parts = [parts[q] + parts[q + 1] for q in range(0, len(parts), 2)]
            a_buf[pl.ds(t0, A_ROWS), lanes] = parts[0]
            return carry

        lax.fori_loop(0, t // A_ROWS, conv_body, 0)

    cv = zcol(3) * zcol(4)
    _stage(cv_buf, 0, cv, C3_BACK, cv_carry[...])
    cv_carry[...] = cv[t - C3_BACK:]
    b_gate = zcol(2)
    for c in range(B_WIDTH // LANES):
        lanes = slice(c * LANES, (c + 1) * LANES)
        conv_b = None
        for k in range(FFN_CONV):
            term = cb_ref[k:k + 1, lanes] * cv_buf[c, k * SUBLANES:k * SUBLANES + t, :]
            conv_b = term if conv_b is None else conv_b + term
        ab_buf[:, A_WIDTH + c * LANES:A_WIDTH + (c + 1) * LANES] = (
            b_gate[:, lanes] * conv_b).astype(jnp.bfloat16)

    for i in range(t // A_ROWS):
        rows = slice(i * A_ROWS, (i + 1) * A_ROWS)
        a = a_buf[rows, :]
        mu = jnp.mean(a, axis=-1, keepdims=True)
        ac = a - mu
        var = jnp.mean(ac * ac, axis=-1, keepdims=True)
        y = ac * lax.rsqrt(var + LN_EPS) * lng_ref[...] + lnb_ref[...]
        ab_buf[rows, 0:A_WIDTH] = _silu(y).astype(jnp.bfloat16)

    return x + _dot(ab_buf[...], w_out_ref[...])


def _pool_mixer(x, g_ref, wp_ref, ps_ref, carry):
    t = SEQ_TILE
    h = _rms(x, g_ref[...])
    row = lax.broadcasted_iota(jnp.int32, (t, LANES), 0)
    pos = pl.program_id(1) * t + (row & (SUBLANES - 1)) * SEG + (row >> SUBLANE_BITS)
    cur = h
    outs = []
    for lvl, w in enumerate(POOL_WINDOWS):
        lo = lvl * POOL_GROUP
        back = (w // 2) * SUBLANES
        tail = cur[t - back:]
        hist = _history_rows(tail, carry[lvl, 0:back, lo:])
        carry[lvl, 0:back, lo:] = tail
        nxt = cur + jnp.concatenate([hist, cur[:t - back]], axis=0)
        inv = 1.0 / jnp.minimum(pos + 1, w).astype(jnp.float32)
        inv = jnp.concatenate([inv] * (POOL_GROUP // LANES), axis=1)
        p = nxt[:, :POOL_GROUP] * inv - h[:, lo:lo + POOL_GROUP]
        outs.append(_dot(p.astype(jnp.bfloat16), wp_ref[lvl]))
        cur = nxt[:, POOL_GROUP:]
    return x + jnp.concatenate(outs, axis=1) * ps_ref[...]


def _conv_ffn(g_ref, w_up_ref, cw_ref, w_down_ref, hb, acc_ref, carry, ubuf, act_buf):
    t = SEQ_TILE
    n = N_FF_CHUNKS
    hb[...] = _rms(acc_ref[...], g_ref[...]).astype(jnp.bfloat16)

    def cols(half, j):
        return half * D_FF + j * FF_CHUNK

    def up(j, slot):
        for half in range(2):
            idx = half * n + j
            u = _dot(hb[...], w_up_ref[:, cols(half, j):cols(half, j) + FF_CHUNK])
            _stage(ubuf[slot], half * FF_PLANES, u, C3_BACK, carry[idx])
            carry[idx] = u[t - C3_BACK:]

    def act(j, slot):
        for rb in range(t // FF_ROWS):
            for c in range(FF_PLANES):
                lanes = slice(c * LANES, (c + 1) * LANES)
                convs = []
                for half in range(2):
                    c0 = cols(half, j) + c * LANES
                    conv = None
                    for k in range(FFN_CONV):
                        r0 = rb * FF_ROWS + k * SUBLANES
                        term = cw_ref[k:k + 1, c0:c0 + LANES] * ubuf[slot][half * FF_PLANES + c, r0:r0 + FF_ROWS, :]
                        conv = term if conv is None else conv + term
                    convs.append(conv)
                gate, val = convs
                act_buf[slot][rb * FF_ROWS:(rb + 1) * FF_ROWS, lanes] = (
                    _silu(gate) * val).astype(jnp.bfloat16)

    def down(j, slot):
        acc_ref[...] += _dot(act_buf[slot][...], w_down_ref[j * FF_CHUNK:(j + 1) * FF_CHUNK, :])

    up(0, 0)
    for j in range(n):
        if j + 1 < n:
            up(j + 1, (j + 1) % 2)
        act(j, j % 2)
        down(j, j % 2)


def _layer0_kernel(x_ref, gm_ref, w_in_ref, ca_ref, lng_ref, lnb_ref, cb_ref, w_out_ref,
                   gf_ref, w_up_ref, cw_ref, w_down_ref, o_ref,
                   xp, glu_buf, glu_carry, a_buf, cv_buf, cv_carry, ab_buf,
                   hb, acc_ref, ffn_carry, u0, u1, a0, a1):
    @pl.when(pl.program_id(1) == 0)
    def _():
        glu_carry[...] = jnp.zeros_like(glu_carry)
        cv_carry[...] = jnp.zeros_like(cv_carry)
        ffn_carry[...] = jnp.zeros_like(ffn_carry)

    for b in range(SEG):
        xp[b * SUBLANES:(b + 1) * SUBLANES, :] = x_ref[0, 0, :, b, :]
    acc_ref[...] = _conv_mixer(xp[...], gm_ref, w_in_ref, ca_ref, lng_ref, lnb_ref, cb_ref, w_out_ref,
                               glu_buf, glu_carry, a_buf, cv_buf, cv_carry, ab_buf)
    _conv_ffn(gf_ref, w_up_ref, cw_ref, w_down_ref, hb, acc_ref, ffn_carry, (u0, u1), (a0, a1))
    o_ref[0] = acc_ref[...]


def _layer1_kernel(x_ref, gm_ref, wp_ref, ps_ref, gf_ref, w_up_ref, cw_ref, w_down_ref, gl_ref, o_ref,
                   pool_carry, hb, acc_ref, ffn_carry, u0, u1, a0, a1):
    @pl.when(pl.program_id(1) == 0)
    def _():
        pool_carry[...] = jnp.zeros_like(pool_carry)
        ffn_carry[...] = jnp.zeros_like(ffn_carry)

    acc_ref[...] = _pool_mixer(x_ref[0], gm_ref, wp_ref, ps_ref, pool_carry)
    _conv_ffn(gf_ref, w_up_ref, cw_ref, w_down_ref, hb, acc_ref, ffn_carry, (u0, u1), (a0, a1))
    acc_ref[...] = _rms(acc_ref[...], gl_ref[...])
    for b in range(SEG):
        o_ref[0, 0, :, b, :] = acc_ref[b * SUBLANES:(b + 1) * SUBLANES, :]


def _param_spec(stacked, layer):
    rest = stacked.shape[1:]
    return pl.BlockSpec((None,) + rest, lambda b, s: (layer,) + (0,) * len(rest),
                        pipeline_mode=pl.Buffered(1))


def _tile_spec(token_order):
    if token_order:
        return pl.BlockSpec((1, 1, SUBLANES, SEG, D_MODEL), lambda b, s: (b, s, 0, 0, 0))
    return pl.BlockSpec((1, SEQ_TILE, D_MODEL), lambda b, s: (b, s, 0))


def _call(body, x, params, scratch, name, in_token_order=False, out_token_order=False):
    batch, seq, d = x.shape
    tiles = seq // SEQ_TILE
    token_shape = (batch, tiles, SUBLANES, SEG, d)
    if in_token_order:
        x = x.reshape(token_shape)
    out = pl.pallas_call(
        body,
        grid=(batch, tiles),
        in_specs=[_tile_spec(in_token_order)] + [_param_spec(p, layer) for p, layer in params],
        out_specs=_tile_spec(out_token_order),
        out_shape=jax.ShapeDtypeStruct(token_shape if out_token_order else (batch, seq, d), x.dtype),
        scratch_shapes=scratch,
        compiler_params=pltpu.CompilerParams(
            dimension_semantics=("arbitrary", "arbitrary"), vmem_limit_bytes=VMEM_LIMIT),
        name=name,
    )(x, *[p for p, _ in params])
    return out.reshape(batch, seq, d)


def _ffn_scratch():
    return [pltpu.VMEM((SEQ_TILE, D_MODEL), jnp.bfloat16),
            pltpu.VMEM((SEQ_TILE, D_MODEL), jnp.float32),
            pltpu.VMEM((2 * N_FF_CHUNKS, C3_BACK, FF_CHUNK), jnp.float32),
            pltpu.VMEM((2 * FF_PLANES, C3_BACK + SEQ_TILE, LANES), jnp.float32),
            pltpu.VMEM((2 * FF_PLANES, C3_BACK + SEQ_TILE, LANES), jnp.float32),
            pltpu.VMEM((SEQ_TILE, FF_CHUNK), jnp.bfloat16),
            pltpu.VMEM((SEQ_TILE, FF_CHUNK), jnp.bfloat16)]


def kernel(x, norm_mix_even, w_in, conv_a, ln_a_g, ln_a_b, conv_b, w_out, norm_mix_odd, w_pool, pool_scale, norm_ffn, w_up, conv_ffn_w, w_down, norm_final):
    def row(v):
        return v.reshape(v.shape[0], 1, v.shape[1])

    bf16 = lambda w: w.astype(jnp.bfloat16)
    norm_ffn_r, w_up_b, w_down_b = row(norm_ffn), bf16(w_up), bf16(w_down)

    def ffn_params(layer):
        return [(norm_ffn_r, layer), (w_up_b, layer), (conv_ffn_w, layer), (w_down_b, layer)]

    params0 = [(row(norm_mix_even), 0), (bf16(w_in), 0), (conv_a, 0), (row(ln_a_g), 0), (row(ln_a_b), 0),
               (conv_b, 0), (bf16(w_out), 0)] + ffn_params(0)
    scratch0 = [pltpu.VMEM((SEQ_TILE, D_MODEL), jnp.float32),
                pltpu.VMEM((A_WIDTH // LANES, A_BACK + SEQ_TILE, LANES), jnp.float32),
                pltpu.VMEM((A_BACK, A_WIDTH), jnp.float32),
                pltpu.VMEM((SEQ_TILE, A_WIDTH), jnp.float32),
                pltpu.VMEM((B_WIDTH // LANES, C3_BACK + SEQ_TILE, LANES), jnp.float32),
                pltpu.VMEM((C3_BACK, B_WIDTH), jnp.float32),
                pltpu.VMEM((SEQ_TILE, D_MODEL), jnp.bfloat16)] + _ffn_scratch()
    x = _call(_layer0_kernel, x, params0, scratch0, "layer0", in_token_order=True)

    params1 = [(row(norm_mix_odd), 0), (bf16(w_pool), 0), (row(pool_scale), 0)] + ffn_params(1) + [
        (norm_final.reshape(1, 1, -1), 0)]
    scratch1 = [pltpu.VMEM((len(POOL_WINDOWS), POOL_BACK, D_MODEL), jnp.float32)] + _ffn_scratch()
    return _call(_layer1_kernel, x, params1, scratch1, "layer1", out_token_order=True)
```

```python
import functools

import jax
import jax.numpy as jnp
from jax import lax
from jax.experimental import pallas as pl
from jax.experimental.pallas import tpu as pltpu

D_MODEL = 1024
A_WIDTH = 512
B_WIDTH = 512
A_CONV = 31
POOL_WINDOWS = (2, 4, 8, 16)
POOL_GROUP = 256
D_FF = 2816
FFN_CONV = 3
RMS_EPS = 1e-6
LN_EPS = 1e-5

SUBLANES = 8
SUBLANE_BITS = SUBLANES.bit_length() - 1
LANES = 128
SEQ_TILE = 512
SEG = SEQ_TILE // SUBLANES
L1_TILES = 2
FF_CHUNK = 256
N_FF_CHUNKS = D_FF // FF_CHUNK
FF_PLANES = FF_CHUNK // LANES
FF_ROWS = 64
A_ROWS = 64
A_PARTIALS = 4
A_BACK = (A_CONV - 1) * SUBLANES
C3_BACK = (FFN_CONV - 1) * SUBLANES
POOL_BACK = (POOL_WINDOWS[-1] // 2) * SUBLANES
VMEM_LIMIT = 58 * 1024 * 1024

_dot = functools.partial(jnp.dot, preferred_element_type=jnp.float32)


def _rms(x, g):
    ms = jnp.mean(x * x, axis=-1, keepdims=True)
    return x * lax.rsqrt(ms + RMS_EPS) * g


def _sigmoid(x):
    return 1.0 / (1.0 + jnp.exp(-x))


def _silu(x):
    hx = 0.5 * x
    return hx + hx * jnp.tanh(hx)


def _history_rows(cur_tail, prev_tail):
    rows = lax.broadcasted_iota(jnp.int32, (SUBLANES, cur_tail.shape[1]), 0)
    groups = []
    for i in range(cur_tail.shape[0] // SUBLANES):
        sl = slice(i * SUBLANES, (i + 1) * SUBLANES)
        groups.append(jnp.where(rows == 0, pltpu.roll(prev_tail[sl], 1, axis=0),
                                pltpu.roll(cur_tail[sl], 1, axis=0)))
    return groups[0] if len(groups) == 1 else jnp.concatenate(groups, axis=0)


def _stage(buf, first_plane, v, back, prev_tail):
    rows = v.shape[0]
    hist = _history_rows(v[rows - back:], prev_tail)
    for c in range(v.shape[1] // LANES):
        lanes = slice(c * LANES, (c + 1) * LANES)
        buf[first_plane + c, 0:back, :] = hist[:, lanes]
        buf[first_plane + c, back:back + rows, :] = v[:, lanes]


def _conv_mixer(x, g_ref, w_in_ref, ca_ref, lng_ref, lnb_ref, cb_ref, w_out_ref,
                glu_buf, glu_carry, a_buf, cv_buf, cv_carry, ab_buf):
    t = SEQ_TILE
    h = _rms(x, g_ref[...]).astype(jnp.bfloat16)

    def zcol(i):
        return _dot(h, w_in_ref[:, i * A_WIDTH:(i + 1) * A_WIDTH])

    glu = zcol(0) * _sigmoid(zcol(1))
    _stage(glu_buf, 0, glu, A_BACK, glu_carry[...])
    glu_carry[...] = glu[t - A_BACK:]

    for c in range(A_WIDTH // LANES):
        lanes = slice(c * LANES, (c + 1) * LANES)

        def conv_body(i, carry, c=c, lanes=lanes):
            t0 = pl.multiple_of(i * A_ROWS, A_ROWS)
            parts = [None] * A_PARTIALS
            for k in range(A_CONV):
                term = ca_ref[k:k + 1, lanes] * glu_buf[c, pl.ds(t0 + k * SUBLANES, A_ROWS), :]
                p = k % A_PARTIALS
                parts[p] = term if parts[p] is None else parts[p] + term
            while len(parts) > 1:
                parts = [parts[q] + parts[q + 1] for q in range(0, len(parts), 2)]
            a_buf[pl.ds(t0, A_ROWS), lanes] = parts[0]
            return carry

        lax.fori_loop(0, t // A_ROWS, conv_body, 0)

    cv = zcol(3) * zcol(4)
    _stage(cv_buf, 0, cv, C3_BACK, cv_carry[...])
    cv_carry[...] = cv[t - C3_BACK:]
    b_gate = zcol(2)
    for c in range(B_WIDTH // LANES):
        lanes = slice(c * LANES, (c + 1) * LANES)
        conv_b = None
        for k in range(FFN_CONV):
            term = cb_ref[k:k + 1, lanes] * cv_buf[c, k * SUBLANES:k * SUBLANES + t, :]
            conv_b = term if conv_b is None else conv_b + term
        ab_buf[:, A_WIDTH + c * LANES:A_WIDTH + (c + 1) * LANES] = (
            b_gate[:, lanes] * conv_b).astype(jnp.bfloat16)

    for i in range(t // A_ROWS):
        rows = slice(i * A_ROWS, (i + 1) * A_ROWS)
        a = a_buf[rows, :]
        mu = jnp.mean(a, axis=-1, keepdims=True)
        ac = a - mu
        var = jnp.mean(ac * ac, axis=-1, keepdims=True)
        y = ac * lax.rsqrt(var + LN_EPS) * lng_ref[...] + lnb_ref[...]
        ab_buf[rows, 0:A_WIDTH] = _silu(y).astype(jnp.bfloat16)

    return x + _dot(ab_buf[...], w_out_ref[...])


def _pool_mixer(x, seq_tile, g_ref, wp_ref, ps_ref, carry):
    t = SEQ_TILE
    h = _rms(x, g_ref[...])
    row = lax.broadcasted_iota(jnp.int32, (t, LANES), 0)
    pos = seq_tile * t + (row & (SUBLANES - 1)) * SEG + (row >> SUBLANE_BITS)
    cur = h
    outs = []
    for lvl, w in enumerate(POOL_WINDOWS):
        lo = lvl * POOL_GROUP
        back = (w // 2) * SUBLANES
        tail = cur[t - back:]
        hist = _history_rows(tail, carry[lvl, 0:back, lo:])
        carry[lvl, 0:back, lo:] = tail
        nxt = cur + jnp.concatenate([hist, cur[:t - back]], axis=0)
        inv = 1.0 / jnp.minimum(pos + 1, w).astype(jnp.float32)
        inv = jnp.concatenate([inv] * (POOL_GROUP // LANES), axis=1)
        p = nxt[:, :POOL_GROUP] * inv - h[:, lo:lo + POOL_GROUP]
        outs.append(_dot(p.astype(jnp.bfloat16), wp_ref[lvl]))
        cur = nxt[:, POOL_GROUP:]
    return x + jnp.concatenate(outs, axis=1) * ps_ref[...]


def _conv_ffn(g_ref, w_up_ref, cw_ref, w_down_ref, hb, acc_ref, carry, ubuf, act_buf):
    t = SEQ_TILE
    n = N_FF_CHUNKS
    n_sub = acc_ref.shape[0] // t
    hb[...] = _rms(acc_ref[...], g_ref[...]).astype(jnp.bfloat16)

    def cols(half, j):
        return half * D_FF + j * FF_CHUNK

    def plane(sub, half, c):
        return (sub * 2 + half) * FF_PLANES + c

    def up(j, slot):
        for half in range(2):
            idx = half * n + j
            u = _dot(hb[...], w_up_ref[:, cols(half, j):cols(half, j) + FF_CHUNK])
            for sub in range(n_sub):
                u_sub = u[sub * t:(sub + 1) * t]
                _stage(ubuf[slot], plane(sub, half, 0), u_sub, C3_BACK, carry[idx])
                carry[idx] = u_sub[t - C3_BACK:]

    def act(j, slot):
        for sub in range(n_sub):
            for rb in range(t // FF_ROWS):
                for c in range(FF_PLANES):
                    lanes = slice(c * LANES, (c + 1) * LANES)
                    convs = []
                    for half in range(2):
                        c0 = cols(half, j) + c * LANES
                        conv = None
                        for k in range(FFN_CONV):
                            r0 = rb * FF_ROWS + k * SUBLANES
                            term = cw_ref[k:k + 1, c0:c0 + LANES] * ubuf[slot][plane(sub, half, c), r0:r0 + FF_ROWS, :]
                            conv = term if conv is None else conv + term
                        convs.append(conv)
                    gate, val = convs
                    o0 = sub * t + rb * FF_ROWS
                    act_buf[slot][o0:o0 + FF_ROWS, lanes] = (_silu(gate) * val).astype(jnp.bfloat16)

    def down(j, slot):
        acc_ref[...] += _dot(act_buf[slot][...], w_down_ref[j * FF_CHUNK:(j + 1) * FF_CHUNK, :])

    up(0, 0)
    for j in range(n):
        if j + 1 < n:
            up(j + 1, (j + 1) % 2)
        act(j, j % 2)
        down(j, j % 2)


def _layer0_kernel(x_ref, gm_ref, w_in_ref, ca_ref, lng_ref, lnb_ref, cb_ref, w_out_ref,
                   gf_ref, w_up_ref, cw_ref, w_down_ref, o_ref,
                   xp, glu_buf, glu_carry, a_buf, cv_buf, cv_carry, ab_buf,
                   hb, acc_ref, ffn_carry, u0, u1, a0, a1):
    @pl.when(pl.program_id(1) == 0)
    def _():
        glu_carry[...] = jnp.zeros_like(glu_carry)
        cv_carry[...] = jnp.zeros_like(cv_carry)
        ffn_carry[...] = jnp.zeros_like(ffn_carry)

    for a in range(SUBLANES):
        for q in range(SEG // SUBLANES):
            for l in range(D_MODEL // LANES):
                xp[l, pl.ds(a + SUBLANES * SUBLANES * q, SUBLANES, stride=SUBLANES), :] = (
                    x_ref[0, 0, a, q * SUBLANES:(q + 1) * SUBLANES, l * LANES:(l + 1) * LANES])
    x_seg = jnp.concatenate([xp[l] for l in range(D_MODEL // LANES)], axis=1)
    acc_ref[...] = _conv_mixer(x_seg, gm_ref, w_in_ref, ca_ref, lng_ref, lnb_ref, cb_ref, w_out_ref,
                               glu_buf, glu_carry, a_buf, cv_buf, cv_carry, ab_buf)
    _conv_ffn(gf_ref, w_up_ref, cw_ref, w_down_ref, hb, acc_ref, ffn_carry, (u0, u1), (a0, a1))
    o_ref[0] = acc_ref[...]


def _layer1_kernel(x_ref, gm_ref, wp_ref, ps_ref, gf_ref, w_up_ref, cw_ref, w_down_ref, gl_ref, o_ref,
                   pool_carry, y_slabs, hb, acc_ref, ffn_carry, u0, u1, a0, a1):
    t = SEQ_TILE

    @pl.when(pl.program_id(1) == 0)
    def _():
        pool_carry[...] = jnp.zeros_like(pool_carry)
        ffn_carry[...] = jnp.zeros_like(ffn_carry)

    for sub in range(L1_TILES):
        rows = slice(sub * t, (sub + 1) * t)
        acc_ref[rows, :] = _pool_mixer(x_ref[0, rows, :], pl.program_id(1) * L1_TILES + sub,
                                       gm_ref, wp_ref, ps_ref, pool_carry)
    _conv_ffn(gf_ref, w_up_ref, cw_ref, w_down_ref, hb, acc_ref, ffn_carry, (u0, u1), (a0, a1))
    y = _rms(acc_ref[...], gl_ref[...])
    for l in range(D_MODEL // LANES):
        y_slabs[l] = y[:, l * LANES:(l + 1) * LANES]
    for sub in range(L1_TILES):
        for a in range(SUBLANES):
            for q in range(SEG // SUBLANES):
                for l in range(D_MODEL // LANES):
                    o_ref[0, sub, a, q * SUBLANES:(q + 1) * SUBLANES, l * LANES:(l + 1) * LANES] = (
                        y_slabs[l, pl.ds(sub * t + a + SUBLANES * SUBLANES * q, SUBLANES, stride=SUBLANES), :])


def _param_spec(stacked, layer):
    rest = stacked.shape[1:]
    return pl.BlockSpec((None,) + rest, lambda b, s: (layer,) + (0,) * len(rest),
                        pipeline_mode=pl.Buffered(1))


def _tile_spec(token_order, tiles_per_step):
    if token_order:
        return pl.BlockSpec((1, tiles_per_step, SUBLANES, SEG, D_MODEL), lambda b, s: (b, s, 0, 0, 0))
    return pl.BlockSpec((1, tiles_per_step * SEQ_TILE, D_MODEL), lambda b, s: (b, s, 0))


def _call(body, x, params, scratch, name, tiles_per_step=1, in_token_order=False, out_token_order=False):
    batch, seq, d = x.shape
    tiles = seq // SEQ_TILE
    token_shape = (batch, tiles, SUBLANES, SEG, d)
    if in_token_order:
        x = x.reshape(token_shape)
    out = pl.pallas_call(
        body,
        grid=(batch, tiles // tiles_per_step),
        in_specs=[_tile_spec(in_token_order, tiles_per_step)] + [_param_spec(p, layer) for p, layer in params],
        out_specs=_tile_spec(out_token_order, tiles_per_step),
        out_shape=jax.ShapeDtypeStruct(token_shape if out_token_order else (batch, seq, d), x.dtype),
        scratch_shapes=scratch,
        compiler_params=pltpu.CompilerParams(
            dimension_semantics=("arbitrary", "arbitrary"), vmem_limit_bytes=VMEM_LIMIT),
        name=name,
    )(x, *[p for p, _ in params])
    return out.reshape(batch, seq, d)


def _ffn_scratch(tiles_per_step):
    rows = tiles_per_step * SEQ_TILE
    ubuf = pltpu.VMEM((tiles_per_step * 2 * FF_PLANES, C3_BACK + SEQ_TILE, LANES), jnp.float32)
    return [pltpu.VMEM((rows, D_MODEL), jnp.bfloat16),
            pltpu.VMEM((rows, D_MODEL), jnp.float32),
            pltpu.VMEM((2 * N_FF_CHUNKS, C3_BACK, FF_CHUNK), jnp.float32),
            ubuf, ubuf,
            pltpu.VMEM((rows, FF_CHUNK), jnp.bfloat16),
            pltpu.VMEM((rows, FF_CHUNK), jnp.bfloat16)]


def kernel(x, norm_mix_even, w_in, conv_a, ln_a_g, ln_a_b, conv_b, w_out, norm_mix_odd, w_pool, pool_scale, norm_ffn, w_up, conv_ffn_w, w_down, norm_final):
    def row(v):
        return v.reshape(v.shape[0], 1, v.shape[1])

    bf16 = lambda w: w.astype(jnp.bfloat16)
    norm_ffn_r, w_up_b, w_down_b = row(norm_ffn), bf16(w_up), bf16(w_down)

    def ffn_params(layer):
        return [(norm_ffn_r, layer), (w_up_b, layer), (conv_ffn_w, layer), (w_down_b, layer)]

    params0 = [(row(norm_mix_even), 0), (bf16(w_in), 0), (conv_a, 0), (row(ln_a_g), 0), (row(ln_a_b), 0),
               (conv_b, 0), (bf16(w_out), 0)] + ffn_params(0)
    scratch0 = [pltpu.VMEM((D_MODEL // LANES, SEQ_TILE, LANES), jnp.float32),
                pltpu.VMEM((A_WIDTH // LANES, A_BACK + SEQ_TILE, LANES), jnp.float32),
                pltpu.VMEM((A_BACK, A_WIDTH), jnp.float32),
                pltpu.VMEM((SEQ_TILE, A_WIDTH), jnp.float32),
                pltpu.VMEM((B_WIDTH // LANES, C3_BACK + SEQ_TILE, LANES), jnp.float32),
                pltpu.VMEM((C3_BACK, B_WIDTH), jnp.float32),
                pltpu.VMEM((SEQ_TILE, D_MODEL), jnp.bfloat16)] + _ffn_scratch(1)
    x = _call(_layer0_kernel, x, params0, scratch0, "layer0", in_token_order=True)

    params1 = [(row(norm_mix_odd), 0), (bf16(w_pool), 0), (row(pool_scale), 0)] + ffn_params(1) + [
        (norm_final.reshape(1, 1, -1), 0)]
    scratch1 = [pltpu.VMEM((len(POOL_WINDOWS), POOL_BACK, D_MODEL), jnp.float32),
                pltpu.VMEM((D_MODEL // LANES, L1_TILES * SEQ_TILE, LANES), jnp.float32)] + _ffn_scratch(L1_TILES)
    return _call(_layer1_kernel, x, params1, scratch1, "layer1", tiles_per_step=L1_TILES, out_token_order=True)
```

```python
import functools

import jax
import jax.numpy as jnp
from jax import lax
from jax.experimental import pallas as pl
from jax.experimental.pallas import tpu as pltpu

D_MODEL = 1024
A_WIDTH = 512
B_WIDTH = 512
A_CONV = 31
POOL_WINDOWS = (2, 4, 8, 16)
POOL_GROUP = 256
D_FF = 2816
FFN_CONV = 3
RMS_EPS = 1e-6
LN_EPS = 1e-5

SUBLANES = 8
SUBLANE_BITS = SUBLANES.bit_length() - 1
LANES = 128
SEQ_TILE = 512
SEG = SEQ_TILE // SUBLANES
L0_TILES = 2
L1_TILES = 2
FF_CHUNK = 256
N_FF_CHUNKS = D_FF // FF_CHUNK
FF_PLANES = FF_CHUNK // LANES
FF_ROWS = 64
A_ROWS = 64
A_PARTIALS = 4
A_BACK = (A_CONV - 1) * SUBLANES
C3_BACK = (FFN_CONV - 1) * SUBLANES
POOL_BACK = (POOL_WINDOWS[-1] // 2) * SUBLANES
VMEM_LIMIT = 58 * 1024 * 1024

_dot = functools.partial(jnp.dot, preferred_element_type=jnp.float32)


def _rms(x, g):
    ms = jnp.mean(x * x, axis=-1, keepdims=True)
    return x * lax.rsqrt(ms + RMS_EPS) * g


def _sigmoid(x):
    return 1.0 / (1.0 + jnp.exp(-x))


def _silu(x):
    hx = 0.5 * x
    return hx + hx * jnp.tanh(hx)


def _history_rows(cur_tail, prev_tail):
    rows = lax.broadcasted_iota(jnp.int32, (SUBLANES, cur_tail.shape[1]), 0)
    groups = []
    for i in range(cur_tail.shape[0] // SUBLANES):
        sl = slice(i * SUBLANES, (i + 1) * SUBLANES)
        groups.append(jnp.where(rows == 0, pltpu.roll(prev_tail[sl], 1, axis=0),
                                pltpu.roll(cur_tail[sl], 1, axis=0)))
    return groups[0] if len(groups) == 1 else jnp.concatenate(groups, axis=0)


def _stage(buf, first_plane, v, back, prev_tail):
    rows = v.shape[0]
    hist = _history_rows(v[rows - back:], prev_tail)
    for c in range(v.shape[1] // LANES):
        lanes = slice(c * LANES, (c + 1) * LANES)
        buf[first_plane + c, 0:back, :] = hist[:, lanes]
        buf[first_plane + c, back:back + rows, :] = v[:, lanes]


def _conv_mixer(x, g_ref, w_in_ref, ca_ref, lng_ref, lnb_ref, cb_ref, w_out_ref,
                glu_buf, glu_carry, a_buf, cv_buf, cv_carry, ab_buf):
    t = SEQ_TILE
    h = _rms(x, g_ref[...]).astype(jnp.bfloat16)

    def zcol(i):
        return _dot(h, w_in_ref[:, i * A_WIDTH:(i + 1) * A_WIDTH])

    glu = zcol(0) * _sigmoid(zcol(1))
    _stage(glu_buf, 0, glu, A_BACK, glu_carry[...])
    glu_carry[...] = glu[t - A_BACK:]

    for c in range(A_WIDTH // LANES):
        lanes = slice(c * LANES, (c + 1) * LANES)

        def conv_body(i, carry, c=c, lanes=lanes):
            t0 = pl.multiple_of(i * A_ROWS, A_ROWS)
            parts = [None] * A_PARTIALS
            for k in range(A_CONV):
                term = ca_ref[k:k + 1, lanes] * glu_buf[c, pl.ds(t0 + k * SUBLANES, A_ROWS), :]
                p = k % A_PARTIALS
                parts[p] = term if parts[p] is None else parts[p] + term
            while len(parts) > 1:
                parts = [parts[q] + parts[q + 1] for q in range(0, len(parts), 2)]
            a_buf[pl.ds(t0, A_ROWS), lanes] = parts[0]
            return carry

        lax.fori_loop(0, t // A_ROWS, conv_body, 0)

    cv = zcol(3) * zcol(4)
    _stage(cv_buf, 0, cv, C3_BACK, cv_carry[...])
    cv_carry[...] = cv[t - C3_BACK:]
    b_gate = zcol(2)
    for c in range(B_WIDTH // LANES):
        lanes = slice(c * LANES, (c + 1) * LANES)
        conv_b = None
        for k in range(FFN_CONV):
            term = cb_ref[k:k + 1, lanes] * cv_buf[c, k * SUBLANES:k * SUBLANES + t, :]
            conv_b = term if conv_b is None else conv_b + term
        ab_buf[:, A_WIDTH + c * LANES:A_WIDTH + (c + 1) * LANES] = (
            b_gate[:, lanes] * conv_b).astype(jnp.bfloat16)

    for i in range(t // A_ROWS):
        rows = slice(i * A_ROWS, (i + 1) * A_ROWS)
        a = a_buf[rows, :]
        mu = jnp.mean(a, axis=-1, keepdims=True)
        ac = a - mu
        var = jnp.mean(ac * ac, axis=-1, keepdims=True)
        y = ac * lax.rsqrt(var + LN_EPS) * lng_ref[...] + lnb_ref[...]
        ab_buf[rows, 0:A_WIDTH] = _silu(y).astype(jnp.bfloat16)

    return x + _dot(ab_buf[...], w_out_ref[...])


def _pool_mixer(x, seq_tile, g_ref, wp_ref, ps_ref, carry):
    t = SEQ_TILE
    h = _rms(x, g_ref[...])
    row = lax.broadcasted_iota(jnp.int32, (t, LANES), 0)
    pos = seq_tile * t + (row & (SUBLANES - 1)) * SEG + (row >> SUBLANE_BITS)
    cur = h
    outs = []
    for lvl, w in enumerate(POOL_WINDOWS):
        lo = lvl * POOL_GROUP
        back = (w // 2) * SUBLANES
        tail = cur[t - back:]
        hist = _history_rows(tail, carry[lvl, 0:back, lo:])
        carry[lvl, 0:back, lo:] = tail
        nxt = cur + jnp.concatenate([hist, cur[:t - back]], axis=0)
        inv = 1.0 / jnp.minimum(pos + 1, w).astype(jnp.float32)
        inv = jnp.concatenate([inv] * (POOL_GROUP // LANES), axis=1)
        p = nxt[:, :POOL_GROUP] * inv - h[:, lo:lo + POOL_GROUP]
        outs.append(_dot(p.astype(jnp.bfloat16), wp_ref[lvl]))
        cur = nxt[:, POOL_GROUP:]
    return x + jnp.concatenate(outs, axis=1) * ps_ref[...]


def _conv_ffn(g_ref, w_up_ref, cw_ref, w_down_ref, hb, acc_ref, carry, ubuf, act_buf):
    t = SEQ_TILE
    n = N_FF_CHUNKS
    n_sub = acc_ref.shape[0] // t
    hb[...] = _rms(acc_ref[...], g_ref[...]).astype(jnp.bfloat16)

    def cols(half, j):
        return half * D_FF + j * FF_CHUNK

    def plane(sub, half, c):
        return (sub * 2 + half) * FF_PLANES + c

    def up(j, slot):
        for half in range(2):
            idx = half * n + j
            u = _dot(hb[...], w_up_ref[:, cols(half, j):cols(half, j) + FF_CHUNK])
            for sub in range(n_sub):
                u_sub = u[sub * t:(sub + 1) * t]
                _stage(ubuf[slot], plane(sub, half, 0), u_sub, C3_BACK, carry[idx])
                carry[idx] = u_sub[t - C3_BACK:]

    def act(j, slot):
        for sub in range(n_sub):
            for rb in range(t // FF_ROWS):
                for c in range(FF_PLANES):
                    lanes = slice(c * LANES, (c + 1) * LANES)
                    convs = []
                    for half in range(2):
                        c0 = cols(half, j) + c * LANES
                        conv = None
                        for k in range(FFN_CONV):
                            r0 = rb * FF_ROWS + k * SUBLANES
                            term = cw_ref[k:k + 1, c0:c0 + LANES] * ubuf[slot][plane(sub, half, c), r0:r0 + FF_ROWS, :]
                            conv = term if conv is None else conv + term
                        convs.append(conv)
                    gate, val = convs
                    o0 = sub * t + rb * FF_ROWS
                    act_buf[slot][o0:o0 + FF_ROWS, lanes] = (_silu(gate) * val).astype(jnp.bfloat16)

    def down(j, slot):
        acc_ref[...] += _dot(act_buf[slot][...], w_down_ref[j * FF_CHUNK:(j + 1) * FF_CHUNK, :])

    up(0, 0)
    for j in range(n):
        if j + 1 < n:
            up(j + 1, (j + 1) % 2)
        act(j, j % 2)
        down(j, j % 2)


def _layer0_kernel(x_ref, gm_ref, w_in_ref, ca_ref, lng_ref, lnb_ref, cb_ref, w_out_ref,
                   gf_ref, w_up_ref, cw_ref, w_down_ref, o_ref,
                   xp, glu_buf, glu_carry, a_buf, cv_buf, cv_carry, ab_buf,
                   hb, acc_ref, ffn_carry, u0, u1, a0, a1):
    @pl.when(pl.program_id(1) == 0)
    def _():
        glu_carry[...] = jnp.zeros_like(glu_carry)
        cv_carry[...] = jnp.zeros_like(cv_carry)
        ffn_carry[...] = jnp.zeros_like(ffn_carry)

    for sub in range(L0_TILES):
        for a in range(SUBLANES):
            for q in range(SEG // SUBLANES):
                for l in range(D_MODEL // LANES):
                    xp[l, pl.ds(a + SUBLANES * SUBLANES * q, SUBLANES, stride=SUBLANES), :] = (
                        x_ref[0, sub, a, q * SUBLANES:(q + 1) * SUBLANES, l * LANES:(l + 1) * LANES])
        x_seg = jnp.concatenate([xp[l] for l in range(D_MODEL // LANES)], axis=1)
        acc_ref[sub * SEQ_TILE:(sub + 1) * SEQ_TILE, :] = _conv_mixer(
            x_seg, gm_ref, w_in_ref, ca_ref, lng_ref, lnb_ref, cb_ref, w_out_ref,
            glu_buf, glu_carry, a_buf, cv_buf, cv_carry, ab_buf)
    _conv_ffn(gf_ref, w_up_ref, cw_ref, w_down_ref, hb, acc_ref, ffn_carry, (u0, u1), (a0, a1))
    o_ref[0] = acc_ref[...]


def _layer1_kernel(x_ref, gm_ref, wp_ref, ps_ref, gf_ref, w_up_ref, cw_ref, w_down_ref, gl_ref, o_ref,
                   pool_carry, y_slabs, hb, acc_ref, ffn_carry, u0, u1, a0, a1):
    t = SEQ_TILE

    @pl.when(pl.program_id(1) == 0)
    def _():
        pool_carry[...] = jnp.zeros_like(pool_carry)
        ffn_carry[...] = jnp.zeros_like(ffn_carry)

    for sub in range(L1_TILES):
        rows = slice(sub * t, (sub + 1) * t)
        acc_ref[rows, :] = _pool_mixer(x_ref[0, rows, :], pl.program_id(1) * L1_TILES + sub,
                                       gm_ref, wp_ref, ps_ref, pool_carry)
    _conv_ffn(gf_ref, w_up_ref, cw_ref, w_down_ref, hb, acc_ref, ffn_carry, (u0, u1), (a0, a1))
    y = _rms(acc_ref[...], gl_ref[...])
    for l in range(D_MODEL // LANES):
        y_slabs[l] = y[:, l * LANES:(l + 1) * LANES]
    for sub in range(L1_TILES):
        for a in range(SUBLANES):
            for q in range(SEG // SUBLANES):
                for l in range(D_MODEL // LANES):
                    o_ref[0, sub, a, q * SUBLANES:(q + 1) * SUBLANES, l * LANES:(l + 1) * LANES] = (
                        y_slabs[l, pl.ds(sub * t + a + SUBLANES * SUBLANES * q, SUBLANES, stride=SUBLANES), :])


def _param_spec(stacked, layer):
    rest = stacked.shape[1:]
    return pl.BlockSpec((None,) + rest, lambda b, s: (layer,) + (0,) * len(rest),
                        pipeline_mode=pl.Buffered(1))


def _tile_spec(token_order, tiles_per_step):
    if token_order:
        return pl.BlockSpec((1, tiles_per_step, SUBLANES, SEG, D_MODEL), lambda b, s: (b, s, 0, 0, 0))
    return pl.BlockSpec((1, tiles_per_step * SEQ_TILE, D_MODEL), lambda b, s: (b, s, 0))


def _call(body, x, params, scratch, name, tiles_per_step=1, in_token_order=False, out_token_order=False):
    batch, seq, d = x.shape
    tiles = seq // SEQ_TILE
    token_shape = (batch, tiles, SUBLANES, SEG, d)
    if in_token_order:
        x = x.reshape(token_shape)
    out = pl.pallas_call(
        body,
        grid=(batch, tiles // tiles_per_step),
        in_specs=[_tile_spec(in_token_order, tiles_per_step)] + [_param_spec(p, layer) for p, layer in params],
        out_specs=_tile_spec(out_token_order, tiles_per_step),
        out_shape=jax.ShapeDtypeStruct(token_shape if out_token_order else (batch, seq, d), x.dtype),
        scratch_shapes=scratch,
        compiler_params=pltpu.CompilerParams(
            dimension_semantics=("arbitrary", "arbitrary"), vmem_limit_bytes=VMEM_LIMIT),
        name=name,
    )(x, *[p for p, _ in params])
    return out.reshape(batch, seq, d)


def _ffn_scratch(tiles_per_step):
    rows = tiles_per_step * SEQ_TILE
    ubuf = pltpu.VMEM((tiles_per_step * 2 * FF_PLANES, C3_BACK + SEQ_TILE, LANES), jnp.float32)
    return [pltpu.VMEM((rows, D_MODEL), jnp.bfloat16),
            pltpu.VMEM((rows, D_MODEL), jnp.float32),
            pltpu.VMEM((2 * N_FF_CHUNKS, C3_BACK, FF_CHUNK), jnp.float32),
            ubuf, ubuf,
            pltpu.VMEM((rows, FF_CHUNK), jnp.bfloat16),
            pltpu.VMEM((rows, FF_CHUNK), jnp.bfloat16)]


def kernel(x, norm_mix_even, w_in, conv_a, ln_a_g, ln_a_b, conv_b, w_out, norm_mix_odd, w_pool, pool_scale, norm_ffn, w_up, conv_ffn_w, w_down, norm_final):
    def row(v):
        return v.reshape(v.shape[0], 1, v.shape[1])

    bf16 = lambda w: w.astype(jnp.bfloat16)
    norm_ffn_r, w_up_b, w_down_b = row(norm_ffn), bf16(w_up), bf16(w_down)

    def ffn_params(layer):
        return [(norm_ffn_r, layer), (w_up_b, layer), (conv_ffn_w, layer), (w_down_b, layer)]

    params0 = [(row(norm_mix_even), 0), (bf16(w_in), 0), (conv_a, 0), (row(ln_a_g), 0), (row(ln_a_b), 0),
               (conv_b, 0), (bf16(w_out), 0)] + ffn_params(0)
    scratch0 = [pltpu.VMEM((D_MODEL // LANES, SEQ_TILE, LANES), jnp.float32),
                pltpu.VMEM((A_WIDTH // LANES, A_BACK + SEQ_TILE, LANES), jnp.float32),
                pltpu.VMEM((A_BACK, A_WIDTH), jnp.float32),
                pltpu.VMEM((SEQ_TILE, A_WIDTH), jnp.float32),
                pltpu.VMEM((B_WIDTH // LANES, C3_BACK + SEQ_TILE, LANES), jnp.float32),
                pltpu.VMEM((C3_BACK, B_WIDTH), jnp.float32),
                pltpu.VMEM((SEQ_TILE, D_MODEL), jnp.bfloat16)] + _ffn_scratch(L0_TILES)
    x = _call(_layer0_kernel, x, params0, scratch0, "layer0", tiles_per_step=L0_TILES, in_token_order=True)

    params1 = [(row(norm_mix_odd), 0), (bf16(w_pool), 0), (row(pool_scale), 0)] + ffn_params(1) + [
        (norm_final.reshape(1, 1, -1), 0)]
    scratch1 = [pltpu.VMEM((len(POOL_WINDOWS), POOL_BACK, D_MODEL), jnp.float32),
                pltpu.VMEM((D_MODEL // LANES, L1_TILES * SEQ_TILE, LANES), jnp.float32)] + _ffn_scratch(L1_TILES)
    return _call(_layer1_kernel, x, params1, scratch1, "layer1", tiles_per_step=L1_TILES, out_token_order=True)
```

```python
import functools

import jax
import jax.numpy as jnp
from jax import lax
from jax.experimental import pallas as pl
from jax.experimental.pallas import tpu as pltpu

D_MODEL = 1024
A_WIDTH = 512
B_WIDTH = 512
A_CONV = 31
POOL_WINDOWS = (2, 4, 8, 16)
POOL_GROUP = 256
D_FF = 2816
FFN_CONV = 3
RMS_EPS = 1e-6
LN_EPS = 1e-5

SUBLANES = 8
SUBLANE_BITS = SUBLANES.bit_length() - 1
LANES = 128
SEQ_TILE = 512
SEG = SEQ_TILE // SUBLANES
L0_TILES = 2
L1_TILES = 2
FF_CHUNK = 256
N_FF_CHUNKS = D_FF // FF_CHUNK
FF_PLANES = FF_CHUNK // LANES
FF_ROWS = 64
A_ROWS = 64
A_PARTIALS = 4
A_BACK = (A_CONV - 1) * SUBLANES
C3_BACK = (FFN_CONV - 1) * SUBLANES
POOL_BACK = (POOL_WINDOWS[-1] // 2) * SUBLANES
VMEM_LIMIT = 58 * 1024 * 1024

_dot = functools.partial(jnp.dot, preferred_element_type=jnp.float32)


def _rms(x, g):
    ms = jnp.mean(x * x, axis=-1, keepdims=True)
    return x * lax.rsqrt(ms + RMS_EPS) * g


def _sigmoid(x):
    return 1.0 / (1.0 + jnp.exp(-x))


def _silu(x):
    hx = 0.5 * x
    return hx + hx * jnp.tanh(hx)


def _history_rows(cur_tail, prev_tail):
    rows = lax.broadcasted_iota(jnp.int32, (SUBLANES, cur_tail.shape[1]), 0)
    groups = []
    for i in range(cur_tail.shape[0] // SUBLANES):
        sl = slice(i * SUBLANES, (i + 1) * SUBLANES)
        groups.append(jnp.where(rows == 0, pltpu.roll(prev_tail[sl], 1, axis=0),
                                pltpu.roll(cur_tail[sl], 1, axis=0)))
    return groups[0] if len(groups) == 1 else jnp.concatenate(groups, axis=0)


def _stage(buf, first_plane, v, back, prev_tail):
    rows = v.shape[0]
    hist = _history_rows(v[rows - back:], prev_tail)
    for c in range(v.shape[1] // LANES):
        lanes = slice(c * LANES, (c + 1) * LANES)
        buf[first_plane + c, 0:back, :] = hist[:, lanes]
        buf[first_plane + c, back:back + rows, :] = v[:, lanes]


def _conv_mixer(x, g_ref, w_in_ref, ca_ref, lng_ref, lnb_ref, cb_ref, w_out_ref,
                glu_buf, glu_carry, a_buf, cv_buf, cv_carry, ab_buf):
    t = SEQ_TILE
    h = _rms(x, g_ref[...]).astype(jnp.bfloat16)

    def zcol(i):
        return _dot(h, w_in_ref[:, i * A_WIDTH:(i + 1) * A_WIDTH])

    glu = zcol(0) * _sigmoid(zcol(1))
    _stage(glu_buf, 0, glu, A_BACK, glu_carry[...])
    glu_carry[...] = glu[t - A_BACK:]

    for c in range(A_WIDTH // LANES):
        lanes = slice(c * LANES, (c + 1) * LANES)

        def conv_body(i, carry, c=c, lanes=lanes):
            t0 = pl.multiple_of(i * A_ROWS, A_ROWS)
            parts = [None] * A_PARTIALS
            for k in range(A_CONV):
                term = ca_ref[k:k + 1, lanes] * glu_buf[c, pl.ds(t0 + k * SUBLANES, A_ROWS), :]
                p = k % A_PARTIALS
                parts[p] = term if parts[p] is None else parts[p] + term
            while len(parts) > 1:
                parts = [parts[q] + parts[q + 1] for q in range(0, len(parts), 2)]
            a_buf[pl.ds(t0, A_ROWS), lanes] = parts[0]
            return carry

        lax.fori_loop(0, t // A_ROWS, conv_body, 0)

    cv = zcol(3) * zcol(4)
    _stage(cv_buf, 0, cv, C3_BACK, cv_carry[...])
    cv_carry[...] = cv[t - C3_BACK:]
    b_gate = zcol(2)
    for c in range(B_WIDTH // LANES):
        lanes = slice(c * LANES, (c + 1) * LANES)
        conv_b = None
        for k in range(FFN_CONV):
            term = cb_ref[k:k + 1, lanes] * cv_buf[c, k * SUBLANES:k * SUBLANES + t, :]
            conv_b = term if conv_b is None else conv_b + term
        ab_buf[:, A_WIDTH + c * LANES:A_WIDTH + (c + 1) * LANES] = (
            b_gate[:, lanes] * conv_b).astype(jnp.bfloat16)

    for i in range(t // A_ROWS):
        rows = slice(i * A_ROWS, (i + 1) * A_ROWS)
        a = a_buf[rows, :]
        mu = jnp.mean(a, axis=-1, keepdims=True)
        ac = a - mu
        var = jnp.mean(ac * ac, axis=-1, keepdims=True)
        y = ac * lax.rsqrt(var + LN_EPS) * lng_ref[...] + lnb_ref[...]
        ab_buf[rows, 0:A_WIDTH] = _silu(y).astype(jnp.bfloat16)

    return x + _dot(ab_buf[...], w_out_ref[...])


def _pool_mixer(x, seq_tile, g_ref, wp_ref, ps_ref, carry):
    t = SEQ_TILE
    h = _rms(x, g_ref[...])
    row = lax.broadcasted_iota(jnp.int32, (t, LANES), 0)
    pos = seq_tile * t + (row & (SUBLANES - 1)) * SEG + (row >> SUBLANE_BITS)
    cur = h
    outs = []
    for lvl, w in enumerate(POOL_WINDOWS):
        lo = lvl * POOL_GROUP
        back = (w // 2) * SUBLANES
        tail = cur[t - back:]
        hist = _history_rows(tail, carry[lvl, 0:back, lo:])
        carry[lvl, 0:back, lo:] = tail
        nxt = cur + jnp.concatenate([hist, cur[:t - back]], axis=0)
        inv = 1.0 / jnp.minimum(pos + 1, w).astype(jnp.float32)
        inv = jnp.concatenate([inv] * (POOL_GROUP // LANES), axis=1)
        p = nxt[:, :POOL_GROUP] * inv - h[:, lo:lo + POOL_GROUP]
        outs.append(_dot(p.astype(jnp.bfloat16), wp_ref[lvl]))
        cur = nxt[:, POOL_GROUP:]
    return x + jnp.concatenate(outs, axis=1) * ps_ref[...]


def _conv_ffn(g_ref, w_up_ref, cw_ref, w_down_ref, hb, acc_ref, carry, ubuf, act_buf):
    t = SEQ_TILE
    n = N_FF_CHUNKS
    n_sub = acc_ref.shape[0] // t
    hb[...] = _rms(acc_ref[...], g_ref[...]).astype(jnp.bfloat16)

    def cols(half, j):
        return half * D_FF + j * FF_CHUNK

    def plane(sub, half, c):
        return (sub * 2 + half) * FF_PLANES + c

    def up(j, slot):
        for half in range(2):
            idx = half * n + j
            u = _dot(hb[...], w_up_ref[:, cols(half, j):cols(half, j) + FF_CHUNK])
            for sub in range(n_sub):
                u_sub = u[sub * t:(sub + 1) * t]
                _stage(ubuf[slot], plane(sub, half, 0), u_sub, C3_BACK, carry[idx])
                carry[idx] = u_sub[t - C3_BACK:]

    def act(j, slot):
        for sub in range(n_sub):
            for rb in range(t // FF_ROWS):
                for c in range(FF_PLANES):
                    lanes = slice(c * LANES, (c + 1) * LANES)
                    convs = []
                    for half in range(2):
                        c0 = cols(half, j) + c * LANES
                        conv = None
                        for k in range(FFN_CONV):
                            r0 = rb * FF_ROWS + k * SUBLANES
                            term = cw_ref[k:k + 1, c0:c0 + LANES] * ubuf[slot][plane(sub, half, c), r0:r0 + FF_ROWS, :]
                            conv = term if conv is None else conv + term
                        convs.append(conv)
                    gate, val = convs
                    o0 = sub * t + rb * FF_ROWS
                    act_buf[slot][o0:o0 + FF_ROWS, lanes] = (_silu(gate) * val).astype(jnp.bfloat16)

    def down(j, slot):
        for h in range(2):
            cs = slice(h * (D_MODEL // 2), (h + 1) * (D_MODEL // 2))
            acc_ref[:, cs] += _dot(act_buf[slot][...], w_down_ref[j * FF_CHUNK:(j + 1) * FF_CHUNK, cs])

    up(0, 0)
    for j in range(n):
        if j + 1 < n:
            up(j + 1, (j + 1) % 2)
        act(j, j % 2)
        down(j, j % 2)


def _layer0_kernel(x_ref, gm_ref, w_in_ref, ca_ref, lng_ref, lnb_ref, cb_ref, w_out_ref,
                   gf_ref, w_up_ref, cw_ref, w_down_ref, o_ref,
                   xp, glu_buf, glu_carry, a_buf, cv_buf, cv_carry, ab_buf,
                   hb, acc_ref, ffn_carry, u0, u1, a0, a1):
    @pl.when(pl.program_id(1) == 0)
    def _():
        glu_carry[...] = jnp.zeros_like(glu_carry)
        cv_carry[...] = jnp.zeros_like(cv_carry)
        ffn_carry[...] = jnp.zeros_like(ffn_carry)

    for sub in range(L0_TILES):
        for a in range(SUBLANES):
            for q in range(SEG // SUBLANES):
                for l in range(D_MODEL // LANES):
                    xp[l, pl.ds(a + SUBLANES * SUBLANES * q, SUBLANES, stride=SUBLANES), :] = (
                        x_ref[0, sub, a, q * SUBLANES:(q + 1) * SUBLANES, l * LANES:(l + 1) * LANES])
        x_seg = jnp.concatenate([xp[l] for l in range(D_MODEL // LANES)], axis=1)
        acc_ref[sub * SEQ_TILE:(sub + 1) * SEQ_TILE, :] = _conv_mixer(
            x_seg, gm_ref, w_in_ref, ca_ref, lng_ref, lnb_ref, cb_ref, w_out_ref,
            glu_buf, glu_carry, a_buf, cv_buf, cv_carry, ab_buf)
    _conv_ffn(gf_ref, w_up_ref, cw_ref, w_down_ref, hb, acc_ref, ffn_carry, (u0, u1), (a0, a1))
    o_ref[0] = acc_ref[...]


def _layer1_kernel(x_ref, gm_ref, wp_ref, ps_ref, gf_ref, w_up_ref, cw_ref, w_down_ref, gl_ref, o_ref,
                   pool_carry, y_slabs, hb, acc_ref, ffn_carry, u0, u1, a0, a1):
    t = SEQ_TILE

    @pl.when(pl.program_id(1) == 0)
    def _():
        pool_carry[...] = jnp.zeros_like(pool_carry)
        ffn_carry[...] = jnp.zeros_like(ffn_carry)

    for sub in range(L1_TILES):
        rows = slice(sub * t, (sub + 1) * t)
        acc_ref[rows, :] = _pool_mixer(x_ref[0, rows, :], pl.program_id(1) * L1_TILES + sub,
                                       gm_ref, wp_ref, ps_ref, pool_carry)
    _conv_ffn(gf_ref, w_up_ref, cw_ref, w_down_ref, hb, acc_ref, ffn_carry, (u0, u1), (a0, a1))
    y = _rms(acc_ref[...], gl_ref[...])
    for l in range(D_MODEL // LANES):
        y_slabs[l] = y[:, l * LANES:(l + 1) * LANES]
    for sub in range(L1_TILES):
        for a in range(SUBLANES):
            for q in range(SEG // SUBLANES):
                for l in range(D_MODEL // LANES):
                    o_ref[0, sub, a, q * SUBLANES:(q + 1) * SUBLANES, l * LANES:(l + 1) * LANES] = (
                        y_slabs[l, pl.ds(sub * t + a + SUBLANES * SUBLANES * q, SUBLANES, stride=SUBLANES), :])


def _param_spec(stacked, layer):
    rest = stacked.shape[1:]
    return pl.BlockSpec((None,) + rest, lambda b, s: (layer,) + (0,) * len(rest),
                        pipeline_mode=pl.Buffered(1))


def _tile_spec(token_order, tiles_per_step):
    if token_order:
        return pl.BlockSpec((1, tiles_per_step, SUBLANES, SEG, D_MODEL), lambda b, s: (b, s, 0, 0, 0))
    return pl.BlockSpec((1, tiles_per_step * SEQ_TILE, D_MODEL), lambda b, s: (b, s, 0))


def _call(body, x, params, scratch, name, tiles_per_step=1, in_token_order=False, out_token_order=False):
    batch, seq, d = x.shape
    tiles = seq // SEQ_TILE
    token_shape = (batch, tiles, SUBLANES, SEG, d)
    if in_token_order:
        x = x.reshape(token_shape)
    out = pl.pallas_call(
        body,
        grid=(batch, tiles // tiles_per_step),
        in_specs=[_tile_spec(in_token_order, tiles_per_step)] + [_param_spec(p, layer) for p, layer in params],
        out_specs=_tile_spec(out_token_order, tiles_per_step),
        out_shape=jax.ShapeDtypeStruct(token_shape if out_token_order else (batch, seq, d), x.dtype),
        scratch_shapes=scratch,
        compiler_params=pltpu.CompilerParams(
            dimension_semantics=("arbitrary", "arbitrary"), vmem_limit_bytes=VMEM_LIMIT),
        name=name,
    )(x, *[p for p, _ in params])
    return out.reshape(batch, seq, d)


def _ffn_scratch(tiles_per_step):
    rows = tiles_per_step * SEQ_TILE
    ubuf = pltpu.VMEM((tiles_per_step * 2 * FF_PLANES, C3_BACK + SEQ_TILE, LANES), jnp.float32)
    return [pltpu.VMEM((rows, D_MODEL), jnp.bfloat16),
            pltpu.VMEM((rows, D_MODEL), jnp.float32),
            pltpu.VMEM((2 * N_FF_CHUNKS, C3_BACK, FF_CHUNK), jnp.float32),
            ubuf, ubuf,
            pltpu.VMEM((rows, FF_CHUNK), jnp.bfloat16),
            pltpu.VMEM((rows, FF_CHUNK), jnp.bfloat16)]


def kernel(x, norm_mix_even, w_in, conv_a, ln_a_g, ln_a_b, conv_b, w_out, norm_mix_odd, w_pool, pool_scale, norm_ffn, w_up, conv_ffn_w, w_down, norm_final):
    def row(v):
        return v.reshape(v.shape[0], 1, v.shape[1])

    bf16 = lambda w: w.astype(jnp.bfloat16)
    norm_ffn_r, w_up_b, w_down_b = row(norm_ffn), bf16(w_up), bf16(w_down)

    def ffn_params(layer):
        return [(norm_ffn_r, layer), (w_up_b, layer), (conv_ffn_w, layer), (w_down_b, layer)]

    params0 = [(row(norm_mix_even), 0), (bf16(w_in), 0), (conv_a, 0), (row(ln_a_g), 0), (row(ln_a_b), 0),
               (conv_b, 0), (bf16(w_out), 0)] + ffn_params(0)
    scratch0 = [pltpu.VMEM((D_MODEL // LANES, SEQ_TILE, LANES), jnp.float32),
                pltpu.VMEM((A_WIDTH // LANES, A_BACK + SEQ_TILE, LANES), jnp.float32),
                pltpu.VMEM((A_BACK, A_WIDTH), jnp.float32),
                pltpu.VMEM((SEQ_TILE, A_WIDTH), jnp.float32),
                pltpu.VMEM((B_WIDTH // LANES, C3_BACK + SEQ_TILE, LANES), jnp.float32),
                pltpu.VMEM((C3_BACK, B_WIDTH), jnp.float32),
                pltpu.VMEM((SEQ_TILE, D_MODEL), jnp.bfloat16)] + _ffn_scratch(L0_TILES)
    x = _call(_layer0_kernel, x, params0, scratch0, "layer0", tiles_per_step=L0_TILES, in_token_order=True)

    params1 = [(row(norm_mix_odd), 0), (bf16(w_pool), 0), (row(pool_scale), 0)] + ffn_params(1) + [
        (norm_final.reshape(1, 1, -1), 0)]
    scratch1 = [pltpu.VMEM((len(POOL_WINDOWS), POOL_BACK, D_MODEL), jnp.float32),
                pltpu.VMEM((D_MODEL // LANES, L1_TILES * SEQ_TILE, LANES), jnp.float32)] + _ffn_scratch(L1_TILES)
    return _call(_layer1_kernel, x, params1, scratch1, "layer1", tiles_per_step=L1_TILES, out_token_order=True)
```

```python
import functools

import jax
import jax.numpy as jnp
from jax import lax
from jax.experimental import pallas as pl
from jax.experimental.pallas import tpu as pltpu

D_MODEL = 1024
A_WIDTH = 512
B_WIDTH = 512
A_CONV = 31
POOL_WINDOWS = (2, 4, 8, 16)
POOL_GROUP = 256
D_FF = 2816
FFN_CONV = 3
RMS_EPS = 1e-6
LN_EPS = 1e-5

SUBLANES = 8
SUBLANE_BITS = SUBLANES.bit_length() - 1
LANES = 128
SEQ_TILE = 512
SEG = SEQ_TILE // SUBLANES
L0_TILES = 2
L1_TILES = 2
FF_CHUNK = 256
N_FF_CHUNKS = D_FF // FF_CHUNK
FF_PLANES = FF_CHUNK // LANES
FF_ROWS = 64
A_ROWS = 64
A_PARTIALS = 4
A_BACK = (A_CONV - 1) * SUBLANES
C3_BACK = (FFN_CONV - 1) * SUBLANES
POOL_BACK = (POOL_WINDOWS[-1] // 2) * SUBLANES
assert POOL_WINDOWS[-1] <= SEG
VMEM_LIMIT = 58 * 1024 * 1024

_dot = functools.partial(jnp.dot, preferred_element_type=jnp.float32)


def _rms(x, g):
    ms = jnp.mean(x * x, axis=-1, keepdims=True)
    return x * lax.rsqrt(ms + RMS_EPS) * g


def _sigmoid(x):
    return 1.0 / (1.0 + jnp.exp(-x))


def _silu(x):
    hx = 0.5 * x
    return hx + hx * jnp.tanh(hx)


def _history_rows(cur_tail, prev_tail):
    rows = lax.broadcasted_iota(jnp.int32, (SUBLANES, cur_tail.shape[1]), 0)
    groups = []
    for i in range(cur_tail.shape[0] // SUBLANES):
        sl = slice(i * SUBLANES, (i + 1) * SUBLANES)
        groups.append(jnp.where(rows == 0, pltpu.roll(prev_tail[sl], 1, axis=0),
                                pltpu.roll(cur_tail[sl], 1, axis=0)))
    return groups[0] if len(groups) == 1 else jnp.concatenate(groups, axis=0)


def _stage(buf, first_plane, v, back, prev_tail):
    rows = v.shape[0]
    hist = _history_rows(v[rows - back:], prev_tail)
    for c in range(v.shape[1] // LANES):
        lanes = slice(c * LANES, (c + 1) * LANES)
        buf[first_plane + c, 0:back, :] = hist[:, lanes]
        buf[first_plane + c, back:back + rows, :] = v[:, lanes]


def _conv_mixer(x, g_ref, w_in_ref, ca_ref, lng_ref, lnb_ref, cb_ref, w_out_ref,
                glu_buf, glu_carry, a_buf, cv_buf, cv_carry, ab_buf):
    t = SEQ_TILE
    h = _rms(x, g_ref[...]).astype(jnp.bfloat16)

    def zcol(i):
        return _dot(h, w_in_ref[:, i * A_WIDTH:(i + 1) * A_WIDTH])

    glu = zcol(0) * _sigmoid(zcol(1))
    _stage(glu_buf, 0, glu, A_BACK, glu_carry[...])
    glu_carry[...] = glu[t - A_BACK:]

    for c in range(A_WIDTH // LANES):
        lanes = slice(c * LANES, (c + 1) * LANES)

        def conv_body(i, carry, c=c, lanes=lanes):
            t0 = pl.multiple_of(i * A_ROWS, A_ROWS)
            parts = [None] * A_PARTIALS
            for k in range(A_CONV):
                term = ca_ref[k:k + 1, lanes] * glu_buf[c, pl.ds(t0 + k * SUBLANES, A_ROWS), :]
                p = k % A_PARTIALS
                parts[p] = term if parts[p] is None else parts[p] + term
            while len(parts) > 1:
                parts = [parts[q] + parts[q + 1] for q in range(0, len(parts), 2)]
            a_buf[pl.ds(t0, A_ROWS), lanes] = parts[0]
            return carry

        lax.fori_loop(0, t // A_ROWS, conv_body, 0)

    cv = zcol(3) * zcol(4)
    _stage(cv_buf, 0, cv, C3_BACK, cv_carry[...])
    cv_carry[...] = cv[t - C3_BACK:]
    b_gate = zcol(2)
    for c in range(B_WIDTH // LANES):
        lanes = slice(c * LANES, (c + 1) * LANES)
        conv_b = None
        for k in range(FFN_CONV):
            term = cb_ref[k:k + 1, lanes] * cv_buf[c, k * SUBLANES:k * SUBLANES + t, :]
            conv_b = term if conv_b is None else conv_b + term
        ab_buf[:, A_WIDTH + c * LANES:A_WIDTH + (c + 1) * LANES] = (
            b_gate[:, lanes] * conv_b).astype(jnp.bfloat16)

    for i in range(t // A_ROWS):
        rows = slice(i * A_ROWS, (i + 1) * A_ROWS)
        a = a_buf[rows, :]
        mu = jnp.mean(a, axis=-1, keepdims=True)
        ac = a - mu
        var = jnp.mean(ac * ac, axis=-1, keepdims=True)
        y = ac * lax.rsqrt(var + LN_EPS) * lng_ref[...] + lnb_ref[...]
        ab_buf[rows, 0:A_WIDTH] = _silu(y).astype(jnp.bfloat16)

    return x + _dot(ab_buf[...], w_out_ref[...])


def _pool_mixer(x, seq_tile, g_ref, wp_ref, ps_ref, carry):
    t = SEQ_TILE
    h = _rms(x, g_ref[...])
    head = POOL_WINDOWS[-1] * SUBLANES
    row = lax.broadcasted_iota(jnp.int32, (head, LANES), 0)
    pos = seq_tile * t + (row & (SUBLANES - 1)) * SEG + (row >> SUBLANE_BITS)
    cur = h
    outs = []
    for lvl, w in enumerate(POOL_WINDOWS):
        lo = lvl * POOL_GROUP
        back = (w // 2) * SUBLANES
        tail = cur[t - back:]
        hist = _history_rows(tail, carry[lvl, 0:back, lo:])
        carry[lvl, 0:back, lo:] = tail
        nxt = cur + jnp.concatenate([hist, cur[:t - back]], axis=0)
        inv = 1.0 / jnp.minimum(pos + 1, w).astype(jnp.float32)
        inv = jnp.concatenate([inv] * (POOL_GROUP // LANES), axis=1)
        s_w = nxt[:, :POOL_GROUP]
        mean = jnp.concatenate([s_w[:head] * inv, s_w[head:] * (1.0 / w)], axis=0)
        p = mean - h[:, lo:lo + POOL_GROUP]
        outs.append(_dot(p.astype(jnp.bfloat16), wp_ref[lvl]))
        cur = nxt[:, POOL_GROUP:]
    return x + jnp.concatenate(outs, axis=1) * ps_ref[...]


def _conv_ffn(g_ref, w_up_ref, cw_ref, w_down_ref, hb, acc_ref, carry, ubuf, act_buf):
    t = SEQ_TILE
    n = N_FF_CHUNKS
    n_sub = acc_ref.shape[0] // t
    hb[...] = _rms(acc_ref[...], g_ref[...]).astype(jnp.bfloat16)

    def cols(half, j):
        return half * D_FF + j * FF_CHUNK

    def plane(sub, half, c):
        return (sub * 2 + half) * FF_PLANES + c

    def up(j, slot):
        for half in range(2):
            idx = half * n + j
            u = _dot(hb[...], w_up_ref[:, cols(half, j):cols(half, j) + FF_CHUNK])
            for sub in range(n_sub):
                u_sub = u[sub * t:(sub + 1) * t]
                _stage(ubuf[slot], plane(sub, half, 0), u_sub, C3_BACK, carry[idx])
                carry[idx] = u_sub[t - C3_BACK:]

    def act(j, slot):
        for sub in range(n_sub):
            for rb in range(t // FF_ROWS):
                for c in range(FF_PLANES):
                    lanes = slice(c * LANES, (c + 1) * LANES)
                    convs = []
                    for half in range(2):
                        c0 = cols(half, j) + c * LANES
                        conv = None
                        for k in range(FFN_CONV):
                            r0 = rb * FF_ROWS + k * SUBLANES
                            term = cw_ref[k:k + 1, c0:c0 + LANES] * ubuf[slot][plane(sub, half, c), r0:r0 + FF_ROWS, :]
                            conv = term if conv is None else conv + term
                        convs.append(conv)
                    gate, val = convs
                    o0 = sub * t + rb * FF_ROWS
                    act_buf[slot][o0:o0 + FF_ROWS, lanes] = (_silu(gate) * val).astype(jnp.bfloat16)

    def down(j, slot):
        acc_ref[...] += _dot(act_buf[slot][...], w_down_ref[j * FF_CHUNK:(j + 1) * FF_CHUNK, :])

    up(0, 0)
    for j in range(n):
        if j + 1 < n:
            up(j + 1, (j + 1) % 2)
        act(j, j % 2)
        down(j, j % 2)


def _layer0_kernel(x_ref, gm_ref, w_in_ref, ca_ref, lng_ref, lnb_ref, cb_ref, w_out_ref,
                   gf_ref, w_up_ref, cw_ref, w_down_ref, o_ref,
                   xp, glu_buf, glu_carry, a_buf, cv_buf, cv_carry, ab_buf,
                   hb, acc_ref, ffn_carry, u0, u1, a0, a1):
    @pl.when(pl.program_id(1) == 0)
    def _():
        glu_carry[...] = jnp.zeros_like(glu_carry)
        cv_carry[...] = jnp.zeros_like(cv_carry)
        ffn_carry[...] = jnp.zeros_like(ffn_carry)

    for sub in range(L0_TILES):
        for a in range(SUBLANES):
            for q in range(SEG // SUBLANES):
                for l in range(D_MODEL // LANES):
                    xp[l, pl.ds(a + SUBLANES * SUBLANES * q, SUBLANES, stride=SUBLANES), :] = (
                        x_ref[0, sub, a, q * SUBLANES:(q + 1) * SUBLANES, l * LANES:(l + 1) * LANES])
        x_seg = jnp.concatenate([xp[l] for l in range(D_MODEL // LANES)], axis=1)
        acc_ref[sub * SEQ_TILE:(sub + 1) * SEQ_TILE, :] = _conv_mixer(
            x_seg, gm_ref, w_in_ref, ca_ref, lng_ref, lnb_ref, cb_ref, w_out_ref,
            glu_buf, glu_carry, a_buf, cv_buf, cv_carry, ab_buf)
    _conv_ffn(gf_ref, w_up_ref, cw_ref, w_down_ref, hb, acc_ref, ffn_carry, (u0, u1), (a0, a1))
    o_ref[0] = acc_ref[...]


def _layer1_kernel(x_ref, gm_ref, wp_ref, ps_ref, gf_ref, w_up_ref, cw_ref, w_down_ref, gl_ref, o_ref,
                   pool_carry, y_slabs, hb, acc_ref, ffn_carry, u0, u1, a0, a1):
    t = SEQ_TILE

    @pl.when(pl.program_id(1) == 0)
    def _():
        pool_carry[...] = jnp.zeros_like(pool_carry)
        ffn_carry[...] = jnp.zeros_like(ffn_carry)

    for sub in range(L1_TILES):
        rows = slice(sub * t, (sub + 1) * t)
        acc_ref[rows, :] = _pool_mixer(x_ref[0, rows, :], pl.program_id(1) * L1_TILES + sub,
                                       gm_ref, wp_ref, ps_ref, pool_carry)
    _conv_ffn(gf_ref, w_up_ref, cw_ref, w_down_ref, hb, acc_ref, ffn_carry, (u0, u1), (a0, a1))
    y = _rms(acc_ref[...], gl_ref[...])
    for l in range(D_MODEL // LANES):
        y_slabs[l] = y[:, l * LANES:(l + 1) * LANES]
    for sub in range(L1_TILES):
        for a in range(SUBLANES):
            for q in range(SEG // SUBLANES):
                for l in range(D_MODEL // LANES):
                    o_ref[0, sub, a, q * SUBLANES:(q + 1) * SUBLANES, l * LANES:(l + 1) * LANES] = (
                        y_slabs[l, pl.ds(sub * t + a + SUBLANES * SUBLANES * q, SUBLANES, stride=SUBLANES), :])


def _param_spec(stacked, layer):
    rest = stacked.shape[1:]
    return pl.BlockSpec((None,) + rest, lambda b, s: (layer,) + (0,) * len(rest),
                        pipeline_mode=pl.Buffered(1))


def _tile_spec(token_order, tiles_per_step):
    if token_order:
        return pl.BlockSpec((1, tiles_per_step, SUBLANES, SEG, D_MODEL), lambda b, s: (b, s, 0, 0, 0))
    return pl.BlockSpec((1, tiles_per_step * SEQ_TILE, D_MODEL), lambda b, s: (b, s, 0))


def _call(body, x, params, scratch, name, tiles_per_step=1, in_token_order=False, out_token_order=False):
    batch, seq, d = x.shape
    tiles = seq // SEQ_TILE
    token_shape = (batch, tiles, SUBLANES, SEG, d)
    if in_token_order:
        x = x.reshape(token_shape)
    out = pl.pallas_call(
        body,
        grid=(batch, tiles // tiles_per_step),
        in_specs=[_tile_spec(in_token_order, tiles_per_step)] + [_param_spec(p, layer) for p, layer in params],
        out_specs=_tile_spec(out_token_order, tiles_per_step),
        out_shape=jax.ShapeDtypeStruct(token_shape if out_token_order else (batch, seq, d), x.dtype),
        scratch_shapes=scratch,
        compiler_params=pltpu.CompilerParams(
            dimension_semantics=("arbitrary", "arbitrary"), vmem_limit_bytes=VMEM_LIMIT),
        name=name,
    )(x, *[p for p, _ in params])
    return out.reshape(batch, seq, d)


def _ffn_scratch(tiles_per_step):
    rows = tiles_per_step * SEQ_TILE
    ubuf = pltpu.VMEM((tiles_per_step * 2 * FF_PLANES, C3_BACK + SEQ_TILE, LANES), jnp.float32)
    return [pltpu.VMEM((rows, D_MODEL), jnp.bfloat16),
            pltpu.VMEM((rows, D_MODEL), jnp.float32),
            pltpu.VMEM((2 * N_FF_CHUNKS, C3_BACK, FF_CHUNK), jnp.float32),
            ubuf, ubuf,
            pltpu.VMEM((rows, FF_CHUNK), jnp.bfloat16),
            pltpu.VMEM((rows, FF_CHUNK), jnp.bfloat16)]


def kernel(x, norm_mix_even, w_in, conv_a, ln_a_g, ln_a_b, conv_b, w_out, norm_mix_odd, w_pool, pool_scale, norm_ffn, w_up, conv_ffn_w, w_down, norm_final):
    def row(v):
        return v.reshape(v.shape[0], 1, v.shape[1])

    bf16 = lambda w: w.astype(jnp.bfloat16)
    norm_ffn_r, w_up_b, w_down_b = row(norm_ffn), bf16(w_up), bf16(w_down)

    def ffn_params(layer):
        return [(norm_ffn_r, layer), (w_up_b, layer), (conv_ffn_w, layer), (w_down_b, layer)]

    params0 = [(row(norm_mix_even), 0), (bf16(w_in), 0), (conv_a, 0), (row(ln_a_g), 0), (row(ln_a_b), 0),
               (conv_b, 0), (bf16(w_out), 0)] + ffn_params(0)
    scratch0 = [pltpu.VMEM((D_MODEL // LANES, SEQ_TILE, LANES), jnp.float32),
                pltpu.VMEM((A_WIDTH // LANES, A_BACK + SEQ_TILE, LANES), jnp.float32),
                pltpu.VMEM((A_BACK, A_WIDTH), jnp.float32),
                pltpu.VMEM((SEQ_TILE, A_WIDTH), jnp.float32),
                pltpu.VMEM((B_WIDTH // LANES, C3_BACK + SEQ_TILE, LANES), jnp.float32),
                pltpu.VMEM((C3_BACK, B_WIDTH), jnp.float32),
                pltpu.VMEM((SEQ_TILE, D_MODEL), jnp.bfloat16)] + _ffn_scratch(L0_TILES)
    x = _call(_layer0_kernel, x, params0, scratch0, "layer0", tiles_per_step=L0_TILES, in_token_order=True)

    params1 = [(row(norm_mix_odd), 0), (bf16(w_pool), 0), (row(pool_scale), 0)] + ffn_params(1) + [
        (norm_final.reshape(1, 1, -1), 0)]
    scratch1 = [pltpu.VMEM((len(POOL_WINDOWS), POOL_BACK, D_MODEL), jnp.float32),
                pltpu.VMEM((D_MODEL // LANES, L1_TILES * SEQ_TILE, LANES), jnp.float32)] + _ffn_scratch(L1_TILES)
    return _call(_layer1_kernel, x, params1, scratch1, "layer1", tiles_per_step=L1_TILES, out_token_order=True)
```
